```python
import jax
import jax.numpy as jnp
from jax import lax
import numpy as np

D_MODEL = 1024
BATCH = 2
SEQ = 8192
DEPTH = 4
DEC_BATCH = 32
DEC_SEQ = 4
PAST_LEN = 8192
PAGE_SIZE = 128

N_MIXERS = 2
N_CONV_LAYERS = (DEPTH + 1) // 2
N_ATTN_LAYERS = DEPTH // 2
CONV_W = 3
N_HEADS = 16
HEAD_DIM = D_MODEL // N_HEADS
BLOCK = 256
TOP_K = 3
Q_BLOCK = 32
D_FF = -(-8 * D_MODEL // 768) * 256
ALPHA = (2 * DEPTH) ** 0.25
BETA = (8 * DEPTH) ** -0.25
LN_EPS = 1e-5

kernel_name = "hybrid_shortconv_moba_decoder_step"


def layer_norm(x, g, b):
    xf = x.astype(jnp.float32)
    mu = jnp.mean(xf, axis=-1, keepdims=True)
    var = jnp.mean(jnp.square(xf - mu), axis=-1, keepdims=True)
    return ((xf - mu) * lax.rsqrt(var + LN_EPS) * g.astype(jnp.float32) + b.astype(jnp.float32)).astype(x.dtype)


def swiglu_ffn(x, w_gate_up, w_down):
    g, u = jnp.split(x @ w_gate_up, 2, axis=-1)
    return (jax.nn.silu(g) * u) @ w_down


def short_conv_mixer(x, prev, w_in, conv_w, w_out):
    t = x.shape[1]
    b_gate, c_gate, h = jnp.split(x @ w_in, 3, axis=-1)
    u = c_gate * h
    u_ext = jnp.concatenate([prev.astype(u.dtype), u], axis=1)
    y = conv_w[0] * u_ext[:, 0:t]
    for j in range(1, CONV_W):
        y = y + conv_w[j] * u_ext[:, j:j + t]
    return (b_gate * y) @ w_out, u_ext[:, -(CONV_W - 1):]


def alibi_slopes():
    return 2.0 ** (-8.0 * jnp.arange(1, N_HEADS + 1, dtype=jnp.float32) / N_HEADS)


def qkv_rows(x, w_qkv):
    b, t, _ = x.shape
    qkv = (x @ w_qkv).reshape(b, t, 3, N_HEADS, HEAD_DIM)
    return qkv[:, :, 0], qkv[:, :, 1], qkv[:, :, 2]


def to_blocks(k_rows, v_rows):
    b, length = k_rows.shape[0], k_rows.shape[1]
    nb = -(-length // BLOCK)
    pad = ((0, 0), (0, 0), (0, nb * BLOCK - length), (0, 0))
    kb = jnp.pad(k_rows.transpose(0, 2, 1, 3), pad).reshape(b, N_HEADS, nb, BLOCK, HEAD_DIM)
    vb = jnp.pad(v_rows.transpose(0, 2, 1, 3), pad).reshape(b, N_HEADS, nb, BLOCK, HEAD_DIM)
    kmean = jnp.mean(kb.astype(jnp.float32), axis=3)
    return kb, vb, kmean


def moba_attend(q, kb, vb, kmean, q_pos, slopes):
    b, nb = kb.shape[0], kb.shape[2]
    k_sel = min(TOP_K, nb)
    own = q_pos // BLOCK
    gate = jnp.einsum('bhqd,bhnd->bhqn', q.astype(jnp.float32), kmean)
    fully_past = jnp.arange(nb, dtype=jnp.int32)[None, :] < own[:, None]
    gate = jnp.where(fully_past, gate, -jnp.inf)
    _, sel = lax.top_k(gate, k_sel)
    valid = sel < own[:, None]
    own_idx = jnp.broadcast_to(own[None, None, :, None], sel.shape[:3] + (1,)).astype(sel.dtype)
    idx = jnp.concatenate([sel, own_idx], axis=-1)
    bi = jnp.arange(b)[:, None, None, None]
    hi = jnp.arange(N_HEADS)[None, :, None, None]
    kg = kb[bi, hi, idx]
    vg = vb[bi, hi, idx]
    s_pos = idx[..., None] * BLOCK + jnp.arange(BLOCK, dtype=idx.dtype)
    t_pos = q_pos[None, None, :, None, None]
    allowed = jnp.concatenate(
        [jnp.broadcast_to(valid[..., None], valid.shape + (BLOCK,)),
         s_pos[:, :, :, -1:, :] <= t_pos], axis=3)
    logits = jnp.einsum('bhqd,bhqjsd->bhqjs', q, kg).astype(jnp.float32) * (HEAD_DIM ** -0.5)
    logits = logits - slopes[None, :, None, None, None] * (t_pos - s_pos).astype(jnp.float32)
    logits = jnp.where(allowed, logits, -jnp.inf)
    shp = logits.shape
    p = jax.nn.softmax(logits.reshape(shp[:3] + (-1,)), axis=-1).reshape(shp)
    return jnp.einsum('bhqjs,bhqjsd->bhqd', p.astype(vg.dtype), vg)


def moba_prompt(x, w_qkv, w_o, slopes):
    b, t, _ = x.shape
    q, k, v = qkv_rows(x, w_qkv)
    kb, vb, kmean = to_blocks(k, v)
    nq = t // Q_BLOCK
    qc = q.transpose(0, 2, 1, 3).reshape(b, N_HEADS, nq, Q_BLOCK, HEAD_DIM).transpose(2, 0, 1, 3, 4)
    pos = jnp.arange(t, dtype=jnp.int32).reshape(nq, Q_BLOCK)
    o = lax.map(lambda a: moba_attend(a[0], kb, vb, kmean, a[1], slopes), (qc, pos))
    o = o.transpose(1, 0, 3, 2, 4).reshape(b, t, D_MODEL)
    return o @ w_o, k, v


def moba_sample(x, cache_k_l, cache_v_l, page_table, w_qkv, w_o, slopes):
    b, s, _ = x.shape
    past = page_table.shape[1] * PAGE_SIZE
    q, k, v = qkv_rows(x, w_qkv)
    past_k = cache_k_l[page_table].reshape(b, past, N_HEADS, HEAD_DIM)
    past_v = cache_v_l[page_table].reshape(b, past, N_HEADS, HEAD_DIM)
    kb, vb, kmean = to_blocks(jnp.concatenate([past_k, k.astype(past_k.dtype)], axis=1),
                              jnp.concatenate([past_v, v.astype(past_v.dtype)], axis=1))
    q_pos = past + jnp.arange(s, dtype=jnp.int32)
    o = moba_attend(q.transpose(0, 2, 1, 3), kb, vb, kmean, q_pos, slopes)
    o = o.transpose(0, 2, 1, 3).reshape(b, s, D_MODEL)
    return o @ w_o, k, v


def setup_inputs(seed: int = 0) -> dict:
    key = jax.random.key(seed)
    ks = jax.random.split(key, 16)
    f32 = jnp.float32
    n_pages = PAST_LEN // PAGE_SIZE
    n_used = DEC_BATCH * n_pages
    n_pool = n_used + n_used // 4

    def w(k, shape, fan_in, scale=1.0):
        return jax.random.normal(k, shape, f32) * (scale * fan_in ** -0.5)

    page_table = jax.random.permutation(ks[5], n_pool)[:n_used].reshape(DEC_BATCH, n_pages).astype(jnp.int32)
    return {
        "x_prompt": jax.random.normal(ks[0], (BATCH, SEQ, D_MODEL), f32),
        "x_sample": jax.random.normal(ks[1], (DEC_BATCH, DEC_SEQ, D_MODEL), f32),
        "cache_k": jax.random.normal(ks[2], (N_ATTN_LAYERS, n_pool, PAGE_SIZE, N_HEADS, HEAD_DIM), f32),
        "cache_v": jax.random.normal(ks[3], (N_ATTN_LAYERS, n_pool, PAGE_SIZE, N_HEADS, HEAD_DIM), f32),
        "state_conv": jax.random.normal(ks[4], (N_CONV_LAYERS, DEC_BATCH, CONV_W - 1, D_MODEL), f32),
        "page_table": page_table,
        "w_in_conv": w(ks[6], (N_CONV_LAYERS, D_MODEL, 3 * D_MODEL), D_MODEL),
        "conv_w": w(ks[7], (N_CONV_LAYERS, CONV_W, D_MODEL), CONV_W),
        "w_out_conv": w(ks[8], (N_CONV_LAYERS, D_MODEL, D_MODEL), D_MODEL, BETA),
        "w_qkv": w(ks[9], (N_ATTN_LAYERS, D_MODEL, 3 * D_MODEL), D_MODEL),
        "w_o": w(ks[10], (N_ATTN_LAYERS, D_MODEL, D_MODEL), D_MODEL, BETA),
        "w_gate_up": w(ks[11], (DEPTH, D_MODEL, 2 * D_FF), D_MODEL),
        "w_down": w(ks[12], (DEPTH, D_FF, D_MODEL), D_FF, BETA),
        "ln_g": 1.0 + 0.02 * jax.random.normal(ks[13], (DEPTH, 2, D_MODEL), f32),
        "ln_b": 0.02 * jax.random.normal(ks[14], (DEPTH, 2, D_MODEL), f32),
    }


def reference(x_prompt, x_sample, cache_k, cache_v, state_conv, page_table, w_in_conv, conv_w, w_out_conv,
              w_qkv, w_o, w_gate_up, w_down, ln_g, ln_b):
    slopes = alibi_slopes()
    xp, xs = x_prompt, x_sample
    k_p, v_p, k_s, v_s, conv_p, conv_s = [], [], [], [], [], []
    for i in range(DEPTH):
        li = i // N_MIXERS
        if i % N_MIXERS == 0:
            zeros = jnp.zeros((xp.shape[0], CONV_W - 1, D_MODEL), xp.dtype)
            mp, cp = short_conv_mixer(xp, zeros, w_in_conv[li], conv_w[li], w_out_conv[li])
            ms, cs = short_conv_mixer(xs, state_conv[li], w_in_conv[li], conv_w[li], w_out_conv[li])
            conv_p.append(cp)
            conv_s.append(cs)
        else:
            mp, kp, vp = moba_prompt(xp, w_qkv[li], w_o[li], slopes)
            ms, ksn, vsn = moba_sample(xs, cache_k[li], cache_v[li], page_table, w_qkv[li], w_o[li], slopes)
            k_p.append(kp)
            v_p.append(vp)
            k_s.append(ksn)
            v_s.append(vsn)
        xp = layer_norm(ALPHA * xp + mp, ln_g[i, 0], ln_b[i, 0])
        xs = layer_norm(ALPHA * xs + ms, ln_g[i, 0], ln_b[i, 0])
        xp = layer_norm(ALPHA * xp + swiglu_ffn(xp, w_gate_up[i], w_down[i]), ln_g[i, 1], ln_b[i, 1])
        xs = layer_norm(ALPHA * xs + swiglu_ffn(xs, w_gate_up[i], w_down[i]), ln_g[i, 1], ln_b[i, 1])
    return (xp, xs, jnp.stack(k_p), jnp.stack(v_p), jnp.stack(conv_p),
            jnp.stack(k_s), jnp.stack(v_s), jnp.stack(conv_s))
```

```python
import functools

import jax
import jax.numpy as jnp
from jax import lax
from jax.experimental import pallas as pl
from jax.experimental.pallas import tpu as pltpu

D_MODEL = 1024
DEPTH = 4
N_HEADS = 16
HEAD_DIM = D_MODEL // N_HEADS
CONV_W = 3
BLOCK = 256
TOP_K = 3
PAGE_SIZE = 128
PAGES_PER_BLOCK = BLOCK // PAGE_SIZE
D_FF = -(-8 * D_MODEL // 768) * 256
ALPHA = (2 * DEPTH) ** 0.25
LN_EPS = 1e-5
NEG = -1e30

VMEM_LIMIT_BYTES = 56 * 1024 * 1024
ROW_TILE = 512
FF_CHUNKS = ((0, 768), (768, 1536), (1536, 2304), (2304, D_FF))

BF16 = jnp.bfloat16
F32 = jnp.float32


def _params(*sem):
    return pltpu.CompilerParams(dimension_semantics=sem, vmem_limit_bytes=VMEM_LIMIT_BYTES)


def _dot(a, b):
    return jnp.dot(a, b, preferred_element_type=F32)


def _dot_nt(a, b, precision=None):
    return lax.dot_general(a, b, (((1,), (1,)), ((), ())), precision=precision,
                           preferred_element_type=F32)


def _layer_norm(z, g, b):
    mu = jnp.mean(z, axis=-1, keepdims=True)
    zc = z - mu
    var = jnp.mean(zc * zc, axis=-1, keepdims=True)
    return zc * lax.rsqrt(var + LN_EPS) * g + b


def _resident(shape):
    zeros = (0,) * len(shape)
    return pl.BlockSpec(shape, lambda *_: zeros)


def _ffn_ln_kernel(x_ref, wgu_ref, wd_ref, g_ref, b_ref, y_ref):
    x = x_ref[...]
    xb = x.astype(BF16)
    acc = ALPHA * x
    for c0, c1 in FF_CHUNKS:
        gate = _dot(xb, wgu_ref[:, c0:c1])
        up = _dot(xb, wgu_ref[:, D_FF + c0:D_FF + c1])
        h = (gate * jax.nn.sigmoid(gate) * up).astype(BF16)
        acc = acc + _dot(h, wd_ref[c0:c1, :])
    y_ref[...] = _layer_norm(acc, g_ref[...], b_ref[...])


def _ffn_ln(x, wgu, wd, g, b):
    n = x.shape[0]
    tm = min(ROW_TILE, n)
    return pl.pallas_call(
        _ffn_ln_kernel,
        grid=(n // tm,),
        in_specs=[pl.BlockSpec((tm, D_MODEL), lambda i: (i, 0)),
                  _resident(wgu.shape), _resident(wd.shape),
                  _resident((1, D_MODEL)), _resident((1, D_MODEL))],
        out_specs=pl.BlockSpec((tm, D_MODEL), lambda i: (i, 0)),
        out_shape=jax.ShapeDtypeStruct((n, D_MODEL), F32),
        compiler_params=_params("parallel"),
        name="ffn_ln",
    )(x, wgu, wd, g, b)


def _proj_ln_kernel(x_ref, a_ref, w_ref, g_ref, b_ref, y_ref, *, a_transposed):
    a = a_ref[0].T if a_transposed else a_ref[...]
    z = ALPHA * x_ref[...] + _dot(a.astype(BF16), w_ref[...])
    y_ref[...] = _layer_norm(z, g_ref[...], b_ref[...])


def _proj_ln(x, a, w, g, b):
    n = x.shape[0]
    tm = min(ROW_TILE, n)
    if a.ndim == 3:
        tiles_per_seq = a.shape[2] // tm
        a_spec = pl.BlockSpec((1, D_MODEL, tm), lambda i: (i // tiles_per_seq, 0, i % tiles_per_seq))
    else:
        a_spec = pl.BlockSpec((tm, D_MODEL), lambda i: (i, 0))
    return pl.pallas_call(
        functools.partial(_proj_ln_kernel, a_transposed=a.ndim == 3),
        grid=(n // tm,),
        in_specs=[pl.BlockSpec((tm, D_MODEL), lambda i: (i, 0)), a_spec,
                  _resident(w.shape), _resident((1, D_MODEL)), _resident((1, D_MODEL))],
        out_specs=pl.BlockSpec((tm, D_MODEL), lambda i: (i, 0)),
        out_shape=jax.ShapeDtypeStruct((n, D_MODEL), F32),
        compiler_params=_params("parallel"),
        name="proj_ln",
    )(x, a, w, g, b)


CARRY = 8


def _conv_prompt_kernel(x_ref, win_ref, cw_ref, wout_ref, g_ref, b_ref, y_ref, state_ref, ubuf,
                        *, tm, tiles_per_seq):
    i = pl.program_id(0)
    x = x_ref[...]
    xb = x.astype(BF16)
    b_gate = _dot(xb, win_ref[:, 0:D_MODEL])
    c_gate = _dot(xb, win_ref[:, D_MODEL:2 * D_MODEL])
    h = _dot(xb, win_ref[:, 2 * D_MODEL:3 * D_MODEL])
    u = c_gate * h

    @pl.when(i % tiles_per_seq == 0)
    def _():
        ubuf[0:CARRY, :] = jnp.zeros((CARRY, D_MODEL), F32)

    @pl.when(i % tiles_per_seq != 0)
    def _():
        ubuf[0:CARRY, :] = ubuf[tm:tm + CARRY, :]

    ubuf[CARRY:CARRY + tm, :] = u
    cw = cw_ref[...]
    y = cw[2:3, :] * u
    for j in range(CONV_W - 1):
        off = CARRY - (CONV_W - 1) + j
        y = y + cw[j:j + 1, :] * ubuf[off:off + tm, :]
    m = _dot((b_gate * y).astype(BF16), wout_ref[...])
    y_ref[...] = _layer_norm(ALPHA * x + m, g_ref[...], b_ref[...])
    state_ref[0] = u[tm - (CONV_W - 1):tm, :]


def _conv_prompt(x, batch, win, cw, wout, g, b):
    n = x.shape[0]
    tm = ROW_TILE
    tiles_per_seq = n // batch // tm
    return pl.pallas_call(
        functools.partial(_conv_prompt_kernel, tm=tm, tiles_per_seq=tiles_per_seq),
        grid=(n // tm,),
        in_specs=[pl.BlockSpec((tm, D_MODEL), lambda i: (i, 0)),
                  _resident(win.shape), _resident(cw.shape), _resident(wout.shape),
                  _resident((1, D_MODEL)), _resident((1, D_MODEL))],
        out_specs=[pl.BlockSpec((tm, D_MODEL), lambda i: (i, 0)),
                   pl.BlockSpec((1, CONV_W - 1, D_MODEL), lambda i: (i // tiles_per_seq, 0, 0))],
        out_shape=[jax.ShapeDtypeStruct((n, D_MODEL), F32),
                   jax.ShapeDtypeStruct((batch, CONV_W - 1, D_MODEL), F32)],
        scratch_shapes=[pltpu.VMEM((tm + CARRY, D_MODEL), F32)],
        compiler_params=_params("arbitrary"),
        name="conv_prompt",
    )(x, win, cw, wout, g, b)


def _conv_sample_kernel(x_ref, p1_ref, p2_ref, win_ref, cw_ref, wout_ref, g_ref, b_ref,
                        y_ref, u_ref, *, seq):
    x = x_ref[...]
    n = x.shape[0]
    xb = x.astype(BF16)
    b_gate = _dot(xb, win_ref[:, 0:D_MODEL])
    c_gate = _dot(xb, win_ref[:, D_MODEL:2 * D_MODEL])
    h = _dot(xb, win_ref[:, 2 * D_MODEL:3 * D_MODEL])
    u = c_gate * h
    pos = lax.broadcasted_iota(jnp.int32, (n, D_MODEL), 0) % seq
    u1 = jnp.where(pos >= 1, pltpu.roll(u, 1, axis=0), p1_ref[...])
    u2 = jnp.where(pos >= 2, pltpu.roll(u, 2, axis=0), p2_ref[...])
    cw = cw_ref[...]
    y = cw[0:1, :] * u2 + cw[1:2, :] * u1 + cw[2:3, :] * u
    m = _dot((b_gate * y).astype(BF16), wout_ref[...])
    y_ref[...] = _layer_norm(ALPHA * x + m, g_ref[...], b_ref[...])
    u_ref[...] = u


def _conv_sample(x, state, seq, win, cw, wout, g, b):
    n = x.shape[0]
    nb = n // seq
    zeros = jnp.zeros((nb, seq, D_MODEL), F32)
    prev1 = zeros.at[:, 0].set(state[:, 1]).reshape(n, D_MODEL)
    prev2 = zeros.at[:, 0].set(state[:, 0]).at[:, 1].set(state[:, 1]).reshape(n, D_MODEL)
    full = _resident((n, D_MODEL))
    y, u = pl.pallas_call(
        functools.partial(_conv_sample_kernel, seq=seq),
        grid=(1,),
        in_specs=[full, full, full, _resident(win.shape), _resident(cw.shape), _resident(wout.shape),
                  _resident((1, D_MODEL)), _resident((1, D_MODEL))],
        out_specs=[full, full],
        out_shape=[jax.ShapeDtypeStruct((n, D_MODEL), F32)] * 2,
        compiler_params=_params("arbitrary"),
        name="conv_sample",
    )(x, prev1, prev2, win, cw, wout, g, b)
    return y, u.reshape(nb, seq, D_MODEL)[:, seq - (CONV_W - 1):]


def _qkv_prompt_kernel(x_ref, w_ref, kf_ref, vf_ref, qt_ref, kh_ref, vt_ref, kmean_ref):
    xb = x_ref[...].astype(BF16)
    q = _dot(xb, w_ref[:, 0:D_MODEL])
    k = _dot(xb, w_ref[:, D_MODEL:2 * D_MODEL])
    v = _dot(xb, w_ref[:, 2 * D_MODEL:3 * D_MODEL])
    kf_ref[...] = k
    vf_ref[...] = v
    kmean_ref[0, 0] = jnp.sum(k, axis=0, keepdims=True) * (1.0 / BLOCK)
    lanes = 2 * HEAD_DIM
    for c in range(N_HEADS // 2):
        qt = q[:, c * lanes:(c + 1) * lanes].T
        vt = v[:, c * lanes:(c + 1) * lanes].T
        for s in range(2):
            hd = 2 * c + s
            qt_ref[0, hd] = qt[s * HEAD_DIM:(s + 1) * HEAD_DIM, :]
            vt_ref[0, hd, 0] = vt[s * HEAD_DIM:(s + 1) * HEAD_DIM, :].astype(BF16)
            kh_ref[0, hd, 0] = k[:, hd * HEAD_DIM:(hd + 1) * HEAD_DIM].astype(BF16)


def _qkv_prompt(x, batch, w):
    n = x.shape[0]
    t = n // batch
    nb = t // BLOCK
    row = pl.BlockSpec((BLOCK, D_MODEL), lambda i: (i, 0))
    return pl.pallas_call(
        _qkv_prompt_kernel,
        grid=(n // BLOCK,),
        in_specs=[row, _resident(w.shape)],
        out_specs=[row, row,
                   pl.BlockSpec((1, N_HEADS, HEAD_DIM, BLOCK), lambda i: (i // nb, 0, 0, i % nb)),
                   pl.BlockSpec((1, N_HEADS, 1, BLOCK, HEAD_DIM), lambda i: (i // nb, 0, i % nb, 0, 0)),
                   pl.BlockSpec((1, N_HEADS, 1, HEAD_DIM, BLOCK), lambda i: (i // nb, 0, i % nb, 0, 0)),
                   pl.BlockSpec((1, 1, 1, D_MODEL), lambda i: (i // nb, i % nb, 0, 0))],
        out_shape=[jax.ShapeDtypeStruct((n, D_MODEL), F32),
                   jax.ShapeDtypeStruct((n, D_MODEL), F32),
                   jax.ShapeDtypeStruct((batch, N_HEADS, HEAD_DIM, t), F32),
                   jax.ShapeDtypeStruct((batch, N_HEADS, nb, BLOCK, HEAD_DIM), BF16),
                   jax.ShapeDtypeStruct((batch, N_HEADS, nb, HEAD_DIM, BLOCK), BF16),
                   jax.ShapeDtypeStruct((batch, nb, 1, D_MODEL), F32)],
        compiler_params=_params("parallel"),
        name="qkv_prompt",
    )(x, w)


def _matmul_kernel(x_ref, w_ref, y_ref):
    y_ref[...] = _dot(x_ref[...].astype(BF16), w_ref[...])


def _matmul(x, w):
    n, m = x.shape[0], w.shape[1]
    return pl.pallas_call(
        _matmul_kernel,
        grid=(1,),
        in_specs=[_resident(x.shape), _resident(w.shape)],
        out_specs=_resident((n, m)),
        out_shape=jax.ShapeDtypeStruct((n, m), F32),
        compiler_params=_params("arbitrary"),
        name="matmul",
    )(x, w)


def _select_topk(gate, blk, axis, n_blocks):
    sel = jnp.zeros(gate.shape, F32)
    for _ in range(TOP_K):
        mx = jnp.max(gate, axis=axis, keepdims=True)
        idx = jnp.min(jnp.where(gate == mx, blk, n_blocks), axis=axis, keepdims=True)
        hit = blk == idx
        sel = jnp.where(hit & (mx > -jnp.inf), 1.0, sel)
        gate = jnp.where(hit, -jnp.inf, gate)
    return sel


def _attn_prompt_kernel(slope_ref, qt_ref, k_ref, vt_ref, kmean_ref, ot_ref, sel_ref):
    i = pl.program_id(2)
    nb = k_ref.shape[2]
    slope = slope_ref[0]
    qt = qt_ref[0, 0]
    gate = jnp.dot(kmean_ref[0, 0], qt, precision=lax.Precision.HIGHEST,
                   preferred_element_type=F32)
    blk = lax.broadcasted_iota(jnp.int32, gate.shape, 0)
    gate = jnp.where(blk < i, gate, -jnp.inf)
    sel_ref[...] = _select_topk(gate, blk, 0, nb)

    qb = (qt * (HEAD_DIM ** -0.5)).astype(BF16)
    key = lax.broadcasted_iota(jnp.int32, (BLOCK, BLOCK), 0)
    qry = lax.broadcasted_iota(jnp.int32, (BLOCK, BLOCK), 1)
    bias = (qry - key).astype(F32) * (-slope)

    s = _dot(k_ref[0, 0, i], qb) + bias
    s = jnp.where(key <= qry, s, NEG)
    m = jnp.max(s, axis=0, keepdims=True)
    p = jnp.exp(s - m)
    l = jnp.sum(p, axis=0, keepdims=True)
    acc = _dot(vt_ref[0, 0, i], p.astype(BF16))

    def body(j, carry):
        m, l, acc = carry
        picked = sel_ref[pl.ds(j, 1), :] > 0.0
        shift = jnp.full((1, BLOCK), (i - j) * BLOCK, jnp.int32).astype(F32) * (-slope)
        s = _dot(k_ref[0, 0, j], qb) + bias + shift
        s = jnp.where(picked, s, NEG)
        m_new = jnp.maximum(m, jnp.max(s, axis=0, keepdims=True))
        alpha = jnp.exp(m - m_new)
        p = jnp.exp(s - m_new)
        l = alpha * l + jnp.sum(p, axis=0, keepdims=True)
        acc = alpha * acc + _dot(vt_ref[0, 0, j], p.astype(BF16))
        return m_new, l, acc

    m, l, acc = lax.fori_loop(0, i, body, (m, l, acc))
    ot_ref[0, 0] = acc / l


def _attn_prompt(slopes, qt, kh, vt, kmean):
    batch, _, _, t = qt.shape
    nb = t // BLOCK
    slope_lanes = jnp.broadcast_to(slopes[:, None, None], (N_HEADS, 1, BLOCK))
    grid_spec = pltpu.PrefetchScalarGridSpec(
        num_scalar_prefetch=0,
        grid=(batch, N_HEADS, nb),
        in_specs=[pl.BlockSpec((1, 1, BLOCK), lambda b, h, i: (h, 0, 0)),
                  pl.BlockSpec((1, 1, HEAD_DIM, BLOCK), lambda b, h, i: (b, h, 0, i)),
                  pl.BlockSpec((1, 1, nb, BLOCK, HEAD_DIM), lambda b, h, i: (b, h, 0, 0, 0)),
                  pl.BlockSpec((1, 1, nb, HEAD_DIM, BLOCK), lambda b, h, i: (b, h, 0, 0, 0)),
                  pl.BlockSpec((1, 1, nb, HEAD_DIM), lambda b, h, i: (b, h, 0, 0))],
        out_specs=pl.BlockSpec((1, 1, HEAD_DIM, BLOCK), lambda b, h, i: (b, h, 0, i)),
        scratch_shapes=[pltpu.VMEM((nb, BLOCK), F32)],
    )
    return pl.pallas_call(
        _attn_prompt_kernel,
        grid_spec=grid_spec,
        out_shape=jax.ShapeDtypeStruct((batch, N_HEADS, HEAD_DIM, t), F32),
        compiler_params=_params("parallel", "parallel", "arbitrary"),
        name="attn_prompt",
    )(slope_lanes, qt, kh, vt, kmean)


def _head_rows(q4, seq):
    rows = jnp.concatenate([jnp.broadcast_to(q4[t:t + 1, :], (N_HEADS, D_MODEL)) for t in range(seq)], axis=0)
    r = lax.broadcasted_iota(jnp.int32, rows.shape, 0)
    lane = lax.broadcasted_iota(jnp.int32, rows.shape, 1)
    return jnp.where(r % N_HEADS == lane // HEAD_DIM, rows, 0.0)


def _sample_scores_kernel(pt_ref, k0_ref, k1_ref, q_ref, knew_ref, slope_ref, p_ref, pown_ref,
                          s_all, kmean_s, *, seq, n_blocks):
    j = pl.program_id(1)
    rows = seq * N_HEADS
    qbd = _head_rows(q_ref[0], seq)
    k0 = k0_ref[0, 0]
    k1 = k1_ref[0, 0]
    ksum = jnp.sum(k0, axis=0, keepdims=True) + jnp.sum(k1, axis=0, keepdims=True)
    kmean_s[pl.ds(j, 1), :] = ksum * (1.0 / BLOCK)
    qb = (qbd * (HEAD_DIM ** -0.5)).astype(BF16)
    s_all[j, :, 0:PAGE_SIZE] = _dot_nt(qb, k0.astype(BF16))
    s_all[j, :, PAGE_SIZE:BLOCK] = _dot_nt(qb, k1.astype(BF16))

    @pl.when(j == n_blocks - 1)
    def _():
        slope = slope_ref[...]
        gate = _dot_nt(qbd, kmean_s[...], precision=lax.Precision.HIGHEST)
        blk = lax.broadcasted_iota(jnp.int32, gate.shape, 1)
        sel = _select_topk(gate, blk, 1, n_blocks)
        past = n_blocks * BLOCK
        tok = lax.broadcasted_iota(jnp.int32, (rows, 1), 0) // N_HEADS
        lane = lax.broadcasted_iota(jnp.int32, (rows, BLOCK), 1)

        knew = knew_ref[0]
        own_lane = lax.broadcasted_iota(jnp.int32, (rows, PAGE_SIZE), 1)
        s_own = jnp.full((rows, PAGE_SIZE), NEG, F32)
        for s in range(seq):
            dots = jnp.sum(qbd * knew[s:s + 1, :], axis=1, keepdims=True) * (HEAD_DIM ** -0.5)
            logit = dots - slope * (tok - s).astype(F32)
            s_own = jnp.where((own_lane == s) & (tok >= s), logit, s_own)
        m = jnp.max(s_own, axis=1, keepdims=True)

        def logits(b):
            dist = (past + tok - b * BLOCK - lane).astype(F32)
            return jnp.where(sel[:, b:b + 1] > 0.0, s_all[b] - slope * dist, NEG)

        for b in range(n_blocks):
            m = jnp.maximum(m, jnp.max(logits(b), axis=1, keepdims=True))
        p_own = jnp.exp(s_own - m)
        l = jnp.sum(p_own, axis=1, keepdims=True)
        for b in range(n_blocks):
            l = l + jnp.sum(jnp.exp(logits(b) - m), axis=1, keepdims=True)
        inv = 1.0 / l
        pown_ref[0] = p_own * inv
        for b in range(n_blocks):
            p_ref[0, b] = (jnp.exp(logits(b) - m) * inv).astype(BF16)


def _sample_scores(page_table, cache, layer, q, knew, slope_rows):
    nbatch, seq, _ = q.shape
    n_pages = page_table.shape[1]
    n_blocks = n_pages // PAGES_PER_BLOCK
    rows = seq * N_HEADS
    pt = page_table.reshape(-1)

    def page_spec(which):
        return pl.BlockSpec((1, 1, PAGE_SIZE, D_MODEL),
                            lambda b, j, pt: (layer, pt[b * n_pages + PAGES_PER_BLOCK * j + which], 0, 0))

    grid_spec = pltpu.PrefetchScalarGridSpec(
        num_scalar_prefetch=1,
        grid=(nbatch, n_blocks),
        in_specs=[page_spec(0), page_spec(1),
                  pl.BlockSpec((1, seq, D_MODEL), lambda b, j, pt: (b, 0, 0)),
                  pl.BlockSpec((1, seq, D_MODEL), lambda b, j, pt: (b, 0, 0)),
                  pl.BlockSpec((rows, 1), lambda b, j, pt: (0, 0))],
        out_specs=[pl.BlockSpec((1, n_blocks, rows, BLOCK), lambda b, j, pt: (b, 0, 0, 0)),
                   pl.BlockSpec((1, rows, PAGE_SIZE), lambda b, j, pt: (b, 0, 0))],
        scratch_shapes=[pltpu.VMEM((n_blocks, rows, BLOCK), F32),
                        pltpu.VMEM((n_blocks, D_MODEL), F32)],
    )
    return pl.pallas_call(
        functools.partial(_sample_scores_kernel, seq=seq, n_blocks=n_blocks),
        grid_spec=grid_spec,
        out_shape=[jax.ShapeDtypeStruct((nbatch, n_blocks, rows, BLOCK), BF16),
                   jax.ShapeDtypeStruct((nbatch, rows, PAGE_SIZE), F32)],
        compiler_params=_params("parallel", "arbitrary"),
        name="sample_scores",
    )(pt, cache, cache, q, knew, slope_rows)


def _sample_values_kernel(pt_ref, v0_ref, v1_ref, p_ref, pown_ref, vnew_ref, o_ref, acc, *, seq, n_blocks):
    j = pl.program_id(1)
    rows = seq * N_HEADS

    @pl.when(j == 0)
    def _():
        acc[...] = jnp.zeros(acc.shape, F32)

    p = p_ref[0, 0]
    acc[...] += (_dot(p[:, 0:PAGE_SIZE], v0_ref[0, 0].astype(BF16))
                 + _dot(p[:, PAGE_SIZE:BLOCK], v1_ref[0, 0].astype(BF16)))

    @pl.when(j == n_blocks - 1)
    def _():
        o = acc[...]
        p_own = pown_ref[0]
        vnew = vnew_ref[0]
        for s in range(seq):
            o = o + p_own[:, s:s + 1] * vnew[s:s + 1, :]
        r = lax.broadcasted_iota(jnp.int32, o.shape, 0)
        lane = lax.broadcasted_iota(jnp.int32, o.shape, 1)
        o = jnp.where(r % N_HEADS == lane // HEAD_DIM, o, 0.0)
        o_ref[0] = jnp.sum(o.reshape(seq, N_HEADS, D_MODEL), axis=1)


def _sample_values(page_table, cache, layer, p, p_own, vnew):
    nbatch, n_blocks, rows, _ = p.shape
    seq = rows // N_HEADS
    n_pages = page_table.shape[1]
    pt = page_table.reshape(-1)

    def page_spec(which):
        return pl.BlockSpec((1, 1, PAGE_SIZE, D_MODEL),
                            lambda b, j, pt: (layer, pt[b * n_pages + PAGES_PER_BLOCK * j + which], 0, 0))

    grid_spec = pltpu.PrefetchScalarGridSpec(
        num_scalar_prefetch=1,
        grid=(nbatch, n_blocks),
        in_specs=[page_spec(0), page_spec(1),
                  pl.BlockSpec((1, 1, rows, BLOCK), lambda b, j, pt: (b, j, 0, 0)),
                  pl.BlockSpec((1, rows, PAGE_SIZE), lambda b, j, pt: (b, 0, 0)),
                  pl.BlockSpec((1, seq, D_MODEL), lambda b, j, pt: (b, 0, 0))],
        out_specs=pl.BlockSpec((1, seq, D_MODEL), lambda b, j, pt: (b, 0, 0)),
        scratch_shapes=[pltpu.VMEM((rows, D_MODEL), F32)],
    )
    return pl.pallas_call(
        functools.partial(_sample_values_kernel, seq=seq, n_blocks=n_blocks),
        grid_spec=grid_spec,
        out_shape=jax.ShapeDtypeStruct((nbatch, seq, D_MODEL), F32),
        compiler_params=_params("parallel", "arbitrary"),
        name="sample_values",
    )(pt, cache, cache, p, p_own, vnew)


def kernel(x_prompt, x_sample, cache_k, cache_v, state_conv, page_table, w_in_conv, conv_w, w_out_conv,
           w_qkv, w_o, w_gate_up, w_down, ln_g, ln_b):
    batch, seq_p, _ = x_prompt.shape
    nbatch, seq_s, _ = x_sample.shape
    n_attn = w_qkv.shape[0]
    n_pool = cache_k.shape[1]
    assert seq_p % ROW_TILE == 0 and seq_p % BLOCK == 0 and seq_s >= CONV_W - 1
    assert (page_table.shape[1] * PAGE_SIZE) % BLOCK == 0

    slopes = 2.0 ** (-8.0 * jnp.arange(1, N_HEADS + 1, dtype=F32) / N_HEADS)
    slope_rows = jnp.tile(slopes, seq_s).reshape(seq_s * N_HEADS, 1)
    cache_k = cache_k.reshape(n_attn, n_pool, PAGE_SIZE, D_MODEL)
    cache_v = cache_v.reshape(n_attn, n_pool, PAGE_SIZE, D_MODEL)

    xp = x_prompt.reshape(batch * seq_p, D_MODEL)
    xs = x_sample.reshape(nbatch * seq_s, D_MODEL)
    k_p, v_p, k_s, v_s, conv_p, conv_s = [], [], [], [], [], []
    for i in range(DEPTH):
        li = i // 2
        g0, b0 = ln_g[i, 0:1], ln_b[i, 0:1]
        g1, b1 = ln_g[i, 1:2], ln_b[i, 1:2]
        if i % 2 == 0:
            win, wout = w_in_conv[li].astype(BF16), w_out_conv[li].astype(BF16)
            xp, cp = _conv_prompt(xp, batch, win, conv_w[li], wout, g0, b0)
            xs, cs = _conv_sample(xs, state_conv[li], seq_s, win, conv_w[li], wout, g0, b0)
            conv_p.append(cp)
            conv_s.append(cs)
        else:
            wqkv, wo = w_qkv[li].astype(BF16), w_o[li].astype(BF16)
            kf, vf, qt, kh, vt, kmean = _qkv_prompt(xp, batch, wqkv)
            kmean = kmean.reshape(batch, -1, N_HEADS, HEAD_DIM).transpose(0, 2, 1, 3)
            ot = _attn_prompt(slopes, qt, kh, vt, kmean)
            xp = _proj_ln(xp, ot.reshape(batch, D_MODEL, seq_p), wo, g0, b0)
            k_p.append(kf.reshape(batch, seq_p, N_HEADS, HEAD_DIM))
            v_p.append(vf.reshape(batch, seq_p, N_HEADS, HEAD_DIM))

            qkv = _matmul(xs, wqkv).reshape(nbatch, seq_s, 3, D_MODEL)
            q, kn, vn = qkv[:, :, 0], qkv[:, :, 1], qkv[:, :, 2]
            p, p_own = _sample_scores(page_table, cache_k, li, q, kn, slope_rows)
            o = _sample_values(page_table, cache_v, li, p, p_own, vn)
            xs = _proj_ln(xs, o.reshape(nbatch * seq_s, D_MODEL), wo, g0, b0)
            k_s.append(kn.reshape(nbatch, seq_s, N_HEADS, HEAD_DIM))
            v_s.append(vn.reshape(nbatch, seq_s, N_HEADS, HEAD_DIM))
        wgu, wd = w_gate_up[i].astype(BF16), w_down[i].astype(BF16)
        xp = _ffn_ln(xp, wgu, wd, g1, b1)
        xs = _ffn_ln(xs, wgu, wd, g1, b1)
    return (xp.reshape(batch, seq_p, D_MODEL), xs.reshape(nbatch, seq_s, D_MODEL),
            jnp.stack(k_p), jnp.stack(v_p), jnp.stack(conv_p),
            jnp.stack(k_s), jnp.stack(v_s), jnp.stack(conv_s))
```

```python
import functools

import jax
import jax.numpy as jnp
from jax import lax
from jax.experimental import pallas as pl
from jax.experimental.pallas import tpu as pltpu

D_MODEL = 1024
DEPTH = 4
N_HEADS = 16
HEAD_DIM = D_MODEL // N_HEADS
CONV_W = 3
BLOCK = 256
TOP_K = 3
PAGE_SIZE = 128
PAGES_PER_BLOCK = BLOCK // PAGE_SIZE
D_FF = -(-8 * D_MODEL // 768) * 256
ALPHA = (2 * DEPTH) ** 0.25
LN_EPS = 1e-5
NEG = -1e30
LOG2E = 1.4426950408889634
HEADS_PER_STEP = 2
PAIR_LANES = HEADS_PER_STEP * HEAD_DIM

VMEM_LIMIT_BYTES = 56 * 1024 * 1024
ROW_TILE = 512
FF_CHUNKS = ((0, 768), (768, 1536), (1536, 2304), (2304, D_FF))

BF16 = jnp.bfloat16
F32 = jnp.float32


def _params(*sem):
    return pltpu.CompilerParams(dimension_semantics=sem, vmem_limit_bytes=VMEM_LIMIT_BYTES)


def _dot(a, b):
    return jnp.dot(a, b, preferred_element_type=F32)


def _dot_nt(a, b, precision=None):
    return lax.dot_general(a, b, (((1,), (1,)), ((), ())), precision=precision,
                           preferred_element_type=F32)


def _layer_norm(z, g, b):
    mu = jnp.mean(z, axis=-1, keepdims=True)
    zc = z - mu
    var = jnp.mean(zc * zc, axis=-1, keepdims=True)
    return zc * lax.rsqrt(var + LN_EPS) * g + b


def _resident(shape):
    zeros = (0,) * len(shape)
    return pl.BlockSpec(shape, lambda *_: zeros)


def _ffn_ln_kernel(x_ref, wgu_ref, wd_ref, g_ref, b_ref, y_ref):
    x = x_ref[...]
    xb = x.astype(BF16)
    acc = ALPHA * x
    for c0, c1 in FF_CHUNKS:
        gate = _dot(xb, wgu_ref[:, c0:c1])
        up = _dot(xb, wgu_ref[:, D_FF + c0:D_FF + c1])
        h = (gate * jax.nn.sigmoid(gate) * up).astype(BF16)
        acc = acc + _dot(h, wd_ref[c0:c1, :])
    y_ref[...] = _layer_norm(acc, g_ref[...], b_ref[...])


def _ffn_ln(x, wgu, wd, g, b):
    n = x.shape[0]
    tm = min(ROW_TILE, n)
    return pl.pallas_call(
        _ffn_ln_kernel,
        grid=(n // tm,),
        in_specs=[pl.BlockSpec((tm, D_MODEL), lambda i: (i, 0)),
                  _resident(wgu.shape), _resident(wd.shape),
                  _resident((1, D_MODEL)), _resident((1, D_MODEL))],
        out_specs=pl.BlockSpec((tm, D_MODEL), lambda i: (i, 0)),
        out_shape=jax.ShapeDtypeStruct((n, D_MODEL), F32),
        compiler_params=_params("parallel"),
        name="ffn_ln",
    )(x, wgu, wd, g, b)


def _proj_ln_kernel(x_ref, a_ref, w_ref, g_ref, b_ref, y_ref, *, a_transposed):
    a = a_ref[0].T if a_transposed else a_ref[...]
    z = ALPHA * x_ref[...] + _dot(a.astype(BF16), w_ref[...])
    y_ref[...] = _layer_norm(z, g_ref[...], b_ref[...])


def _proj_ln(x, a, w, g, b):
    n = x.shape[0]
    tm = min(ROW_TILE, n)
    if a.ndim == 3:
        tiles_per_seq = a.shape[2] // tm
        a_spec = pl.BlockSpec((1, D_MODEL, tm), lambda i: (i // tiles_per_seq, 0, i % tiles_per_seq))
    else:
        a_spec = pl.BlockSpec((tm, D_MODEL), lambda i: (i, 0))
    return pl.pallas_call(
        functools.partial(_proj_ln_kernel, a_transposed=a.ndim == 3),
        grid=(n // tm,),
        in_specs=[pl.BlockSpec((tm, D_MODEL), lambda i: (i, 0)), a_spec,
                  _resident(w.shape), _resident((1, D_MODEL)), _resident((1, D_MODEL))],
        out_specs=pl.BlockSpec((tm, D_MODEL), lambda i: (i, 0)),
        out_shape=jax.ShapeDtypeStruct((n, D_MODEL), F32),
        compiler_params=_params("parallel"),
        name="proj_ln",
    )(x, a, w, g, b)


CARRY = 8


def _conv_prompt_kernel(x_ref, win_ref, cw_ref, wout_ref, g_ref, b_ref, y_ref, state_ref, ubuf,
                        *, tm, tiles_per_seq):
    i = pl.program_id(0)
    x = x_ref[...]
    xb = x.astype(BF16)
    b_gate = _dot(xb, win_ref[:, 0:D_MODEL])
    c_gate = _dot(xb, win_ref[:, D_MODEL:2 * D_MODEL])
    h = _dot(xb, win_ref[:, 2 * D_MODEL:3 * D_MODEL])
    u = c_gate * h

    @pl.when(i % tiles_per_seq == 0)
    def _():
        ubuf[0:CARRY, :] = jnp.zeros((CARRY, D_MODEL), F32)

    @pl.when(i % tiles_per_seq != 0)
    def _():
        ubuf[0:CARRY, :] = ubuf[tm:tm + CARRY, :]

    ubuf[CARRY:CARRY + tm, :] = u
    cw = cw_ref[...]
    y = cw[2:3, :] * u
    for j in range(CONV_W - 1):
        off = CARRY - (CONV_W - 1) + j
        y = y + cw[j:j + 1, :] * ubuf[off:off + tm, :]
    m = _dot((b_gate * y).astype(BF16), wout_ref[...])
    y_ref[...] = _layer_norm(ALPHA * x + m, g_ref[...], b_ref[...])
    state_ref[0] = u[tm - (CONV_W - 1):tm, :]


def _conv_prompt(x, batch, win, cw, wout, g, b):
    n = x.shape[0]
    tm = ROW_TILE
    tiles_per_seq = n // batch // tm
    return pl.pallas_call(
        functools.partial(_conv_prompt_kernel, tm=tm, tiles_per_seq=tiles_per_seq),
        grid=(n // tm,),
        in_specs=[pl.BlockSpec((tm, D_MODEL), lambda i: (i, 0)),
                  _resident(win.shape), _resident(cw.shape), _resident(wout.shape),
                  _resident((1, D_MODEL)), _resident((1, D_MODEL))],
        out_specs=[pl.BlockSpec((tm, D_MODEL), lambda i: (i, 0)),
                   pl.BlockSpec((1, CONV_W - 1, D_MODEL), lambda i: (i // tiles_per_seq, 0, 0))],
        out_shape=[jax.ShapeDtypeStruct((n, D_MODEL), F32),
                   jax.ShapeDtypeStruct((batch, CONV_W - 1, D_MODEL), F32)],
        scratch_shapes=[pltpu.VMEM((tm + CARRY, D_MODEL), F32)],
        compiler_params=_params("arbitrary"),
        name="conv_prompt",
    )(x, win, cw, wout, g, b)


def _conv_sample_kernel(x_ref, p1_ref, p2_ref, win_ref, cw_ref, wout_ref, g_ref, b_ref,
                        y_ref, u_ref, *, seq):
    x = x_ref[...]
    n = x.shape[0]
    xb = x.astype(BF16)
    b_gate = _dot(xb, win_ref[:, 0:D_MODEL])
    c_gate = _dot(xb, win_ref[:, D_MODEL:2 * D_MODEL])
    h = _dot(xb, win_ref[:, 2 * D_MODEL:3 * D_MODEL])
    u = c_gate * h
    pos = lax.broadcasted_iota(jnp.int32, (n, D_MODEL), 0) % seq
    u1 = jnp.where(pos >= 1, pltpu.roll(u, 1, axis=0), p1_ref[...])
    u2 = jnp.where(pos >= 2, pltpu.roll(u, 2, axis=0), p2_ref[...])
    cw = cw_ref[...]
    y = cw[0:1, :] * u2 + cw[1:2, :] * u1 + cw[2:3, :] * u
    m = _dot((b_gate * y).astype(BF16), wout_ref[...])
    y_ref[...] = _layer_norm(ALPHA * x + m, g_ref[...], b_ref[...])
    u_ref[...] = u


def _conv_sample(x, state, seq, win, cw, wout, g, b):
    n = x.shape[0]
    nb = n // seq
    zeros = jnp.zeros((nb, seq, D_MODEL), F32)
    prev1 = zeros.at[:, 0].set(state[:, 1]).reshape(n, D_MODEL)
    prev2 = zeros.at[:, 0].set(state[:, 0]).at[:, 1].set(state[:, 1]).reshape(n, D_MODEL)
    full = _resident((n, D_MODEL))
    y, u = pl.pallas_call(
        functools.partial(_conv_sample_kernel, seq=seq),
        grid=(1,),
        in_specs=[full, full, full, _resident(win.shape), _resident(cw.shape), _resident(wout.shape),
                  _resident((1, D_MODEL)), _resident((1, D_MODEL))],
        out_specs=[full, full],
        out_shape=[jax.ShapeDtypeStruct((n, D_MODEL), F32)] * 2,
        compiler_params=_params("arbitrary"),
        name="conv_sample",
    )(x, prev1, prev2, win, cw, wout, g, b)
    return y, u.reshape(nb, seq, D_MODEL)[:, seq - (CONV_W - 1):]


def _qkv_prompt_kernel(x_ref, w_ref, kf_ref, vf_ref, qt_ref, kh_ref, vt_ref, kmean_ref):
    xb = x_ref[...].astype(BF16)
    q = _dot(xb, w_ref[:, 0:D_MODEL])
    k = _dot(xb, w_ref[:, D_MODEL:2 * D_MODEL])
    v = _dot(xb, w_ref[:, 2 * D_MODEL:3 * D_MODEL])
    kf_ref[...] = k
    vf_ref[...] = v
    kmean_ref[0, 0] = jnp.sum(k, axis=0, keepdims=True) * (1.0 / BLOCK)
    for c in range(N_HEADS // HEADS_PER_STEP):
        lanes = slice(c * PAIR_LANES, (c + 1) * PAIR_LANES)
        qt = q[:, lanes].T
        vt = v[:, lanes].T
        kh_ref[0, c, 0] = k[:, lanes].astype(BF16)
        for s in range(HEADS_PER_STEP):
            hd = HEADS_PER_STEP * c + s
            qt_ref[0, hd] = qt[s * HEAD_DIM:(s + 1) * HEAD_DIM, :]
            vt_ref[0, hd, 0] = vt[s * HEAD_DIM:(s + 1) * HEAD_DIM, :].astype(BF16)


def _qkv_prompt(x, batch, w):
    n = x.shape[0]
    t = n // batch
    nb = t // BLOCK
    pairs = N_HEADS // HEADS_PER_STEP
    row = pl.BlockSpec((BLOCK, D_MODEL), lambda i: (i, 0))
    return pl.pallas_call(
        _qkv_prompt_kernel,
        grid=(n // BLOCK,),
        in_specs=[row, _resident(w.shape)],
        out_specs=[row, row,
                   pl.BlockSpec((1, N_HEADS, HEAD_DIM, BLOCK), lambda i: (i // nb, 0, 0, i % nb)),
                   pl.BlockSpec((1, pairs, 1, BLOCK, PAIR_LANES), lambda i: (i // nb, 0, i % nb, 0, 0)),
                   pl.BlockSpec((1, N_HEADS, 1, HEAD_DIM, BLOCK), lambda i: (i // nb, 0, i % nb, 0, 0)),
                   pl.BlockSpec((1, 1, 1, D_MODEL), lambda i: (i // nb, i % nb, 0, 0))],
        out_shape=[jax.ShapeDtypeStruct((n, D_MODEL), F32),
                   jax.ShapeDtypeStruct((n, D_MODEL), F32),
                   jax.ShapeDtypeStruct((batch, N_HEADS, HEAD_DIM, t), F32),
                   jax.ShapeDtypeStruct((batch, pairs, nb, BLOCK, PAIR_LANES), BF16),
                   jax.ShapeDtypeStruct((batch, N_HEADS, nb, HEAD_DIM, BLOCK), BF16),
                   jax.ShapeDtypeStruct((batch, nb, 1, D_MODEL), F32)],
        compiler_params=_params("parallel"),
        name="qkv_prompt",
    )(x, w)


def _matmul_kernel(x_ref, w_ref, y_ref):
    y_ref[...] = _dot(x_ref[...].astype(BF16), w_ref[...])


def _matmul(x, w):
    n, m = x.shape[0], w.shape[1]
    return pl.pallas_call(
        _matmul_kernel,
        grid=(1,),
        in_specs=[_resident(x.shape), _resident(w.shape)],
        out_specs=_resident((n, m)),
        out_shape=jax.ShapeDtypeStruct((n, m), F32),
        compiler_params=_params("arbitrary"),
        name="matmul",
    )(x, w)


def _select_topk(gate, blk, axis, n_blocks):
    sel = jnp.zeros(gate.shape, F32)
    for _ in range(TOP_K):
        mx = jnp.max(gate, axis=axis, keepdims=True)
        idx = jnp.min(jnp.where(gate == mx, blk, n_blocks), axis=axis, keepdims=True)
        hit = blk == idx
        sel = jnp.where(hit & (mx > -jnp.inf), 1.0, sel)
        gate = jnp.where(hit, -jnp.inf, gate)
    return sel


def _attn_prompt_kernel(slope_ref, qt_ref, k_ref, vt_ref, kmean_ref, ot_ref, sel_ref, slot_a, slot_b):
    i = pl.program_id(2)
    nb = k_ref.shape[2]
    key = lax.broadcasted_iota(jnp.int32, (BLOCK, BLOCK), 0)
    qry = lax.broadcasted_iota(jnp.int32, (BLOCK, BLOCK), 1)
    rel = (qry - key).astype(F32)
    blk = lax.broadcasted_iota(jnp.int32, (nb, BLOCK), 0)
    zeros = jnp.zeros((HEAD_DIM, BLOCK), BF16)

    weights, biases = [], []
    for s in range(HEADS_PER_STEP):
        slope2 = slope_ref[s] * LOG2E
        qt = qt_ref[0, s]
        gate = jnp.dot(kmean_ref[0, s], qt, precision=lax.Precision.HIGHEST,
                       preferred_element_type=F32)
        gate = jnp.where(blk < i, gate, -jnp.inf)
        sel = _select_topk(gate, blk, 0, nb)
        shift = (i - blk).astype(F32) * (-BLOCK * slope2)
        sel_ref[s] = jnp.where(sel > 0.0, shift, NEG)
        qb = (qt * (HEAD_DIM ** -0.5 * LOG2E)).astype(BF16)
        weights.append(jnp.concatenate([qb, zeros] if s == 0 else [zeros, qb], axis=0))
        biases.append(rel * (-slope2))

    def produce(j, slot, causal=False):
        kj = k_ref[0, 0, jnp.minimum(j, nb - 1)]
        maxima = []
        for s in range(HEADS_PER_STEP):
            t = _dot(kj, weights[s]) + biases[s]
            if causal:
                t = jnp.where(key <= qry, t, NEG)
            slot[s] = t
            maxima.append(jnp.max(t, axis=0, keepdims=True))
        return maxima

    def consume(j, slot, maxima, rows, state):
        out = []
        for s in range(HEADS_PER_STEP):
            m, l, acc = state[3 * s:3 * s + 3]
            m_new = jnp.maximum(m, maxima[s] + rows[s])
            alpha = jnp.exp2(m - m_new)
            p = jnp.exp2(slot[s] + (rows[s] - m_new))
            l = alpha * l + jnp.sum(p, axis=0, keepdims=True)
            acc = alpha * acc + _dot(vt_ref[0, s, j], p.astype(BF16))
            out += [m_new, l, acc]
        return out

    def rows_of(j):
        return [sel_ref[s, pl.ds(j, 1), :] for s in range(HEADS_PER_STEP)]

    state = []
    for s in range(HEADS_PER_STEP):
        state += [jnp.full((1, BLOCK), NEG, F32), jnp.zeros((1, BLOCK), F32), jnp.zeros((HEAD_DIM, BLOCK), F32)]
    max_diag = produce(i, slot_b, causal=True)
    max_a = produce(0, slot_a)
    no_shift = [jnp.zeros((1, BLOCK), F32)] * HEADS_PER_STEP
    state = consume(i, slot_b, max_diag, no_shift, state)
    n_state = len(state)

    def body(g, carry):
        state, max_a = list(carry[:n_state]), list(carry[n_state:])
        j = 2 * g
        max_b = produce(j + 1, slot_b)
        state = consume(j, slot_a, max_a, rows_of(j), state)
        max_a = produce(j + 2, slot_a)
        state = consume(jnp.minimum(j + 1, nb - 1), slot_b, max_b, rows_of(j + 1), state)
        return tuple(state) + tuple(max_a)

    carry = lax.fori_loop(0, (i + 1) // 2, body, tuple(state) + tuple(max_a))
    for s in range(HEADS_PER_STEP):
        ot_ref[0, s] = carry[3 * s + 2] / carry[3 * s + 1]


def _attn_prompt(slopes, qt, kh, vt, kmean):
    batch, _, _, t = qt.shape
    nb = t // BLOCK
    assert nb % 2 == 0
    hps = HEADS_PER_STEP
    slope_lanes = jnp.broadcast_to(slopes[:, None, None], (N_HEADS, 1, BLOCK))
    return pl.pallas_call(
        _attn_prompt_kernel,
        grid=(batch, N_HEADS // hps, nb),
        in_specs=[pl.BlockSpec((hps, 1, BLOCK), lambda b, h, i: (h, 0, 0)),
                  pl.BlockSpec((1, hps, HEAD_DIM, BLOCK), lambda b, h, i: (b, h, 0, i)),
                  pl.BlockSpec((1, 1, nb, BLOCK, PAIR_LANES), lambda b, h, i: (b, h, 0, 0, 0)),
                  pl.BlockSpec((1, hps, nb, HEAD_DIM, BLOCK), lambda b, h, i: (b, h, 0, 0, 0)),
                  pl.BlockSpec((1, hps, nb, HEAD_DIM), lambda b, h, i: (b, h, 0, 0))],
        out_specs=pl.BlockSpec((1, hps, HEAD_DIM, BLOCK), lambda b, h, i: (b, h, 0, i)),
        out_shape=jax.ShapeDtypeStruct((batch, N_HEADS, HEAD_DIM, t), F32),
        scratch_shapes=[pltpu.VMEM((hps, nb, BLOCK), F32),
                        pltpu.VMEM((hps, BLOCK, BLOCK), F32),
                        pltpu.VMEM((hps, BLOCK, BLOCK), F32)],
        compiler_params=_params("parallel", "parallel", "arbitrary"),
        name="attn_prompt",
    )(slope_lanes, qt, kh, vt, kmean)


def _page_rows(page_ref):
    heads = [page_ref[0, 0, pl.ds(h, PAGE_SIZE, stride=N_HEADS), :] for h in range(N_HEADS)]
    return jnp.concatenate(heads, axis=1)


def _page_spec(layer, n_pages, which):
    return pl.BlockSpec((1, 1, PAGE_SIZE * N_HEADS, HEAD_DIM),
                        lambda b, j, pt: (layer, pt[b * n_pages + PAGES_PER_BLOCK * j + which], 0, 0))


def _head_rows(q4, seq):
    rows = jnp.concatenate([jnp.broadcast_to(q4[t:t + 1, :], (N_HEADS, D_MODEL)) for t in range(seq)], axis=0)
    r = lax.broadcasted_iota(jnp.int32, rows.shape, 0)
    lane = lax.broadcasted_iota(jnp.int32, rows.shape, 1)
    return jnp.where(r % N_HEADS == lane // HEAD_DIM, rows, 0.0)


def _sample_scores_kernel(pt_ref, k0_ref, k1_ref, q_ref, knew_ref, slope_ref, p_ref, pown_ref,
                          s_all, kmean_s, *, seq, n_blocks):
    j = pl.program_id(1)
    rows = seq * N_HEADS
    qbd = _head_rows(q_ref[0], seq)
    qb = (qbd * (HEAD_DIM ** -0.5)).astype(BF16)
    k0 = _page_rows(k0_ref)
    k1 = _page_rows(k1_ref)
    ksum = jnp.sum(k0, axis=0, keepdims=True) + jnp.sum(k1, axis=0, keepdims=True)
    kmean_s[pl.ds(j, 1), :] = ksum * (1.0 / BLOCK)
    s_all[j, :, 0:PAGE_SIZE] = _dot_nt(qb, k0.astype(BF16))
    s_all[j, :, PAGE_SIZE:BLOCK] = _dot_nt(qb, k1.astype(BF16))

    @pl.when(j == n_blocks - 1)
    def _():
        slope = slope_ref[...]
        gate = _dot_nt(qbd, kmean_s[...], precision=lax.Precision.HIGHEST)
        blk = lax.broadcasted_iota(jnp.int32, gate.shape, 1)
        sel = _select_topk(gate, blk, 1, n_blocks)
        past = n_blocks * BLOCK
        tok = lax.broadcasted_iota(jnp.int32, (rows, 1), 0) // N_HEADS
        lane = lax.broadcasted_iota(jnp.int32, (rows, BLOCK), 1)

        knew = knew_ref[0]
        own_lane = lax.broadcasted_iota(jnp.int32, (rows, PAGE_SIZE), 1)
        s_own = jnp.full((rows, PAGE_SIZE), NEG, F32)
        for s in range(seq):
            dots = jnp.sum(qbd * knew[s:s + 1, :], axis=1, keepdims=True) * (HEAD_DIM ** -0.5)
            logit = dots - slope * (tok - s).astype(F32)
            s_own = jnp.where((own_lane == s) & (tok >= s), logit, s_own)
        m = jnp.max(s_own, axis=1, keepdims=True)

        def logits(b):
            dist = (past + tok - b * BLOCK - lane).astype(F32)
            return jnp.where(sel[:, b:b + 1] > 0.0, s_all[b] - slope * dist, NEG)

        for b in range(n_blocks):
            m = jnp.maximum(m, jnp.max(logits(b), axis=1, keepdims=True))
        p_own = jnp.exp(s_own - m)
        l = jnp.sum(p_own, axis=1, keepdims=True)
        for b in range(n_blocks):
            l = l + jnp.sum(jnp.exp(logits(b) - m), axis=1, keepdims=True)
        inv = 1.0 / l
        pown_ref[0] = p_own * inv
        for b in range(n_blocks):
            p_ref[0, b] = (jnp.exp(logits(b) - m) * inv).astype(BF16)


def _sample_scores(page_table, cache, layer, q, knew, slope_rows):
    nbatch, seq, _ = q.shape
    n_pages = page_table.shape[1]
    n_blocks = n_pages // PAGES_PER_BLOCK
    rows = seq * N_HEADS
    grid_spec = pltpu.PrefetchScalarGridSpec(
        num_scalar_prefetch=1,
        grid=(nbatch, n_blocks),
        in_specs=[_page_spec(layer, n_pages, 0), _page_spec(layer, n_pages, 1),
                  pl.BlockSpec((1, seq, D_MODEL), lambda b, j, pt: (b, 0, 0)),
                  pl.BlockSpec((1, seq, D_MODEL), lambda b, j, pt: (b, 0, 0)),
                  pl.BlockSpec((rows, 1), lambda b, j, pt: (0, 0))],
        out_specs=[pl.BlockSpec((1, n_blocks, rows, BLOCK), lambda b, j, pt: (b, 0, 0, 0)),
                   pl.BlockSpec((1, rows, PAGE_SIZE), lambda b, j, pt: (b, 0, 0))],
        scratch_shapes=[pltpu.VMEM((n_blocks, rows, BLOCK), F32),
                        pltpu.VMEM((n_blocks, D_MODEL), F32)],
    )
    return pl.pallas_call(
        functools.partial(_sample_scores_kernel, seq=seq, n_blocks=n_blocks),
        grid_spec=grid_spec,
        out_shape=[jax.ShapeDtypeStruct((nbatch, n_blocks, rows, BLOCK), BF16),
                   jax.ShapeDtypeStruct((nbatch, rows, PAGE_SIZE), F32)],
        compiler_params=_params("parallel", "arbitrary"),
        name="sample_scores",
    )(page_table.reshape(-1), cache, cache, q, knew, slope_rows)


def _sample_values_kernel(pt_ref, v0_ref, v1_ref, p_ref, pown_ref, vnew_ref, o_ref, acc, *, seq, n_blocks):
    j = pl.program_id(1)

    @pl.when(j == 0)
    def _():
        acc[...] = jnp.zeros(acc.shape, F32)

    p = p_ref[0, 0]
    acc[...] += (_dot(p[:, 0:PAGE_SIZE], _page_rows(v0_ref).astype(BF16))
                 + _dot(p[:, PAGE_SIZE:BLOCK], _page_rows(v1_ref).astype(BF16)))

    @pl.when(j == n_blocks - 1)
    def _():
        o = acc[...]
        p_own = pown_ref[0]
        vnew = vnew_ref[0]
        for s in range(seq):
            o = o + p_own[:, s:s + 1] * vnew[s:s + 1, :]
        r = lax.broadcasted_iota(jnp.int32, o.shape, 0)
        lane = lax.broadcasted_iota(jnp.int32, o.shape, 1)
        o = jnp.where(r % N_HEADS == lane // HEAD_DIM, o, 0.0)
        o_ref[0] = jnp.sum(o.reshape(seq, N_HEADS, D_MODEL), axis=1)


def _sample_values(page_table, cache, layer, p, p_own, vnew):
    nbatch, n_blocks, rows, _ = p.shape
    seq = rows // N_HEADS
    n_pages = page_table.shape[1]
    grid_spec = pltpu.PrefetchScalarGridSpec(
        num_scalar_prefetch=1,
        grid=(nbatch, n_blocks),
        in_specs=[_page_spec(layer, n_pages, 0), _page_spec(layer, n_pages, 1),
                  pl.BlockSpec((1, 1, rows, BLOCK), lambda b, j, pt: (b, j, 0, 0)),
                  pl.BlockSpec((1, rows, PAGE_SIZE), lambda b, j, pt: (b, 0, 0)),
                  pl.BlockSpec((1, seq, D_MODEL), lambda b, j, pt: (b, 0, 0))],
        out_specs=pl.BlockSpec((1, seq, D_MODEL), lambda b, j, pt: (b, 0, 0)),
        scratch_shapes=[pltpu.VMEM((rows, D_MODEL), F32)],
    )
    return pl.pallas_call(
        functools.partial(_sample_values_kernel, seq=seq, n_blocks=n_blocks),
        grid_spec=grid_spec,
        out_shape=jax.ShapeDtypeStruct((nbatch, seq, D_MODEL), F32),
        compiler_params=_params("parallel", "arbitrary"),
        name="sample_values",
    )(page_table.reshape(-1), cache, cache, p, p_own, vnew)


def kernel(x_prompt, x_sample, cache_k, cache_v, state_conv, page_table, w_in_conv, conv_w, w_out_conv,
           w_qkv, w_o, w_gate_up, w_down, ln_g, ln_b):
    batch, seq_p, _ = x_prompt.shape
    nbatch, seq_s, _ = x_sample.shape
    n_attn = w_qkv.shape[0]
    n_pool = cache_k.shape[1]
    assert seq_p % ROW_TILE == 0 and seq_p % BLOCK == 0 and seq_s >= CONV_W - 1
    assert (page_table.shape[1] * PAGE_SIZE) % BLOCK == 0

    slopes = 2.0 ** (-8.0 * jnp.arange(1, N_HEADS + 1, dtype=F32) / N_HEADS)
    slope_rows = jnp.tile(slopes, seq_s).reshape(seq_s * N_HEADS, 1)
    cache_k = cache_k.reshape(n_attn, n_pool, PAGE_SIZE * N_HEADS, HEAD_DIM)
    cache_v = cache_v.reshape(n_attn, n_pool, PAGE_SIZE * N_HEADS, HEAD_DIM)

    xp = x_prompt.reshape(batch * seq_p, D_MODEL)
    xs = x_sample.reshape(nbatch * seq_s, D_MODEL)
    k_p, v_p, k_s, v_s, conv_p, conv_s = [], [], [], [], [], []
    for i in range(DEPTH):
        li = i // 2
        g0, b0 = ln_g[i, 0:1], ln_b[i, 0:1]
        g1, b1 = ln_g[i, 1:2], ln_b[i, 1:2]
        if i % 2 == 0:
            win, wout = w_in_conv[li].astype(BF16), w_out_conv[li].astype(BF16)
            xp, cp = _conv_prompt(xp, batch, win, conv_w[li], wout, g0, b0)
            xs, cs = _conv_sample(xs, state_conv[li], seq_s, win, conv_w[li], wout, g0, b0)
            conv_p.append(cp)
            conv_s.append(cs)
        else:
            wqkv, wo = w_qkv[li].astype(BF16), w_o[li].astype(BF16)
            kf, vf, qt, kh, vt, kmean = _qkv_prompt(xp, batch, wqkv)
            kmean = kmean.reshape(batch, -1, N_HEADS, HEAD_DIM).transpose(0, 2, 1, 3)
            ot = _attn_prompt(slopes, qt, kh, vt, kmean)
            xp = _proj_ln(xp, ot.reshape(batch, D_MODEL, seq_p), wo, g0, b0)
            k_p.append(kf.reshape(batch, seq_p, N_HEADS, HEAD_DIM))
            v_p.append(vf.reshape(batch, seq_p, N_HEADS, HEAD_DIM))

            qkv = _matmul(xs, wqkv).reshape(nbatch, seq_s, 3, D_MODEL)
            q, kn, vn = qkv[:, :, 0], qkv[:, :, 1], qkv[:, :, 2]
            p, p_own = _sample_scores(page_table, cache_k, li, q, kn, slope_rows)
            o = _sample_values(page_table, cache_v, li, p, p_own, vn)
            xs = _proj_ln(xs, o.reshape(nbatch * seq_s, D_MODEL), wo, g0, b0)
            k_s.append(kn.reshape(nbatch, seq_s, N_HEADS, HEAD_DIM))
            v_s.append(vn.reshape(nbatch, seq_s, N_HEADS, HEAD_DIM))
        wgu, wd = w_gate_up[i].astype(BF16), w_down[i].astype(BF16)
        xp = _ffn_ln(xp, wgu, wd, g1, b1)
        xs = _ffn_ln(xs, wgu, wd, g1, b1)
    return (xp.reshape(batch, seq_p, D_MODEL), xs.reshape(nbatch, seq_s, D_MODEL),
            jnp.stack(k_p), jnp.stack(v_p), jnp.stack(conv_p),
            jnp.stack(k_s), jnp.stack(v_s), jnp.stack(conv_s))
```

```python
import functools

import jax
import jax.numpy as jnp
from jax import lax
from jax.experimental import pallas as pl
from jax.experimental.pallas import tpu as pltpu

D_MODEL = 1024
DEPTH = 4
N_HEADS = 16
HEAD_DIM = D_MODEL // N_HEADS
CONV_W = 3
BLOCK = 256
TOP_K = 3
PAGE_SIZE = 128
PAGES_PER_BLOCK = BLOCK // PAGE_SIZE
D_FF = -(-8 * D_MODEL // 768) * 256
ALPHA = (2 * DEPTH) ** 0.25
LN_EPS = 1e-5
NEG = -1e30
LOG2E = 1.4426950408889634
HEADS_PER_STEP = 2
PAIR_LANES = HEADS_PER_STEP * HEAD_DIM
TRIP_STAGES = 4

VMEM_LIMIT_BYTES = 56 * 1024 * 1024
ROW_TILE = 512
FF_CHUNKS = ((0, 768), (768, 1536), (1536, 2304), (2304, D_FF))

BF16 = jnp.bfloat16
F32 = jnp.float32


def _params(*sem):
    return pltpu.CompilerParams(dimension_semantics=sem, vmem_limit_bytes=VMEM_LIMIT_BYTES)


def _dot(a, b):
    return jnp.dot(a, b, preferred_element_type=F32)


def _dot_nt(a, b, precision=None):
    return lax.dot_general(a, b, (((1,), (1,)), ((), ())), precision=precision,
                           preferred_element_type=F32)


def _layer_norm(z, g, b):
    mu = jnp.mean(z, axis=-1, keepdims=True)
    zc = z - mu
    var = jnp.mean(zc * zc, axis=-1, keepdims=True)
    return zc * lax.rsqrt(var + LN_EPS) * g + b


def _resident(shape):
    zeros = (0,) * len(shape)
    return pl.BlockSpec(shape, lambda *_: zeros)


def _ffn_ln_kernel(x_ref, wgu_ref, wd_ref, g_ref, b_ref, y_ref):
    x = x_ref[...]
    xb = x.astype(BF16)
    acc = ALPHA * x
    for c0, c1 in FF_CHUNKS:
        gate = _dot(xb, wgu_ref[:, c0:c1])
        up = _dot(xb, wgu_ref[:, D_FF + c0:D_FF + c1])
        h = (gate * jax.nn.sigmoid(gate) * up).astype(BF16)
        acc = acc + _dot(h, wd_ref[c0:c1, :])
    y_ref[...] = _layer_norm(acc, g_ref[...], b_ref[...])


def _ffn_ln(x, wgu, wd, g, b):
    n = x.shape[0]
    tm = min(ROW_TILE, n)
    return pl.pallas_call(
        _ffn_ln_kernel,
        grid=(n // tm,),
        in_specs=[pl.BlockSpec((tm, D_MODEL), lambda i: (i, 0)),
                  _resident(wgu.shape), _resident(wd.shape),
                  _resident((1, D_MODEL)), _resident((1, D_MODEL))],
        out_specs=pl.BlockSpec((tm, D_MODEL), lambda i: (i, 0)),
        out_shape=jax.ShapeDtypeStruct((n, D_MODEL), F32),
        compiler_params=_params("parallel"),
        name="ffn_ln",
    )(x, wgu, wd, g, b)


def _proj_ln_kernel(x_ref, a_ref, w_ref, g_ref, b_ref, y_ref, *, a_transposed):
    a = a_ref[0].T if a_transposed else a_ref[...]
    z = ALPHA * x_ref[...] + _dot(a.astype(BF16), w_ref[...])
    y_ref[...] = _layer_norm(z, g_ref[...], b_ref[...])


def _proj_ln(x, a, w, g, b):
    n = x.shape[0]
    tm = min(ROW_TILE, n)
    if a.ndim == 3:
        tiles_per_seq = a.shape[2] // tm
        a_spec = pl.BlockSpec((1, D_MODEL, tm), lambda i: (i // tiles_per_seq, 0, i % tiles_per_seq))
    else:
        a_spec = pl.BlockSpec((tm, D_MODEL), lambda i: (i, 0))
    return pl.pallas_call(
        functools.partial(_proj_ln_kernel, a_transposed=a.ndim == 3),
        grid=(n // tm,),
        in_specs=[pl.BlockSpec((tm, D_MODEL), lambda i: (i, 0)), a_spec,
                  _resident(w.shape), _resident((1, D_MODEL)), _resident((1, D_MODEL))],
        out_specs=pl.BlockSpec((tm, D_MODEL), lambda i: (i, 0)),
        out_shape=jax.ShapeDtypeStruct((n, D_MODEL), F32),
        compiler_params=_params("parallel"),
        name="proj_ln",
    )(x, a, w, g, b)


CARRY = 8


def _conv_prompt_kernel(x_ref, win_ref, cw_ref, wout_ref, g_ref, b_ref, y_ref, state_ref, ubuf,
                        *, tm, tiles_per_seq):
    i = pl.program_id(0)
    x = x_ref[...]
    xb = x.astype(BF16)
    b_gate = _dot(xb, win_ref[:, 0:D_MODEL])
    c_gate = _dot(xb, win_ref[:, D_MODEL:2 * D_MODEL])
    h = _dot(xb, win_ref[:, 2 * D_MODEL:3 * D_MODEL])
    u = c_gate * h

    @pl.when(i % tiles_per_seq == 0)
    def _():
        ubuf[0:CARRY, :] = jnp.zeros((CARRY, D_MODEL), F32)

    @pl.when(i % tiles_per_seq != 0)
    def _():
        ubuf[0:CARRY, :] = ubuf[tm:tm + CARRY, :]

    ubuf[CARRY:CARRY + tm, :] = u
    cw = cw_ref[...]
    y = cw[2:3, :] * u
    for j in range(CONV_W - 1):
        off = CARRY - (CONV_W - 1) + j
        y = y + cw[j:j + 1, :] * ubuf[off:off + tm, :]
    m = _dot((b_gate * y).astype(BF16), wout_ref[...])
    y_ref[...] = _layer_norm(ALPHA * x + m, g_ref[...], b_ref[...])
    state_ref[0] = u[tm - (CONV_W - 1):tm, :]


def _conv_prompt(x, batch, win, cw, wout, g, b):
    n = x.shape[0]
    tm = ROW_TILE
    tiles_per_seq = n // batch // tm
    return pl.pallas_call(
        functools.partial(_conv_prompt_kernel, tm=tm, tiles_per_seq=tiles_per_seq),
        grid=(n // tm,),
        in_specs=[pl.BlockSpec((tm, D_MODEL), lambda i: (i, 0)),
                  _resident(win.shape), _resident(cw.shape), _resident(wout.shape),
                  _resident((1, D_MODEL)), _resident((1, D_MODEL))],
        out_specs=[pl.BlockSpec((tm, D_MODEL), lambda i: (i, 0)),
                   pl.BlockSpec((1, CONV_W - 1, D_MODEL), lambda i: (i // tiles_per_seq, 0, 0))],
        out_shape=[jax.ShapeDtypeStruct((n, D_MODEL), F32),
                   jax.ShapeDtypeStruct((batch, CONV_W - 1, D_MODEL), F32)],
        scratch_shapes=[pltpu.VMEM((tm + CARRY, D_MODEL), F32)],
        compiler_params=_params("arbitrary"),
        name="conv_prompt",
    )(x, win, cw, wout, g, b)


def _conv_sample_kernel(x_ref, p1_ref, p2_ref, win_ref, cw_ref, wout_ref, g_ref, b_ref,
                        y_ref, u_ref, *, seq):
    x = x_ref[...]
    n = x.shape[0]
    xb = x.astype(BF16)
    b_gate = _dot(xb, win_ref[:, 0:D_MODEL])
    c_gate = _dot(xb, win_ref[:, D_MODEL:2 * D_MODEL])
    h = _dot(xb, win_ref[:, 2 * D_MODEL:3 * D_MODEL])
    u = c_gate * h
    pos = lax.broadcasted_iota(jnp.int32, (n, D_MODEL), 0) % seq
    u1 = jnp.where(pos >= 1, pltpu.roll(u, 1, axis=0), p1_ref[...])
    u2 = jnp.where(pos >= 2, pltpu.roll(u, 2, axis=0), p2_ref[...])
    cw = cw_ref[...]
    y = cw[0:1, :] * u2 + cw[1:2, :] * u1 + cw[2:3, :] * u
    m = _dot((b_gate * y).astype(BF16), wout_ref[...])
    y_ref[...] = _layer_norm(ALPHA * x + m, g_ref[...], b_ref[...])
    u_ref[...] = u


def _conv_sample(x, state, seq, win, cw, wout, g, b):
    n = x.shape[0]
    nb = n // seq
    zeros = jnp.zeros((nb, seq, D_MODEL), F32)
    prev1 = zeros.at[:, 0].set(state[:, 1]).reshape(n, D_MODEL)
    prev2 = zeros.at[:, 0].set(state[:, 0]).at[:, 1].set(state[:, 1]).reshape(n, D_MODEL)
    full = _resident((n, D_MODEL))
    y, u = pl.pallas_call(
        functools.partial(_conv_sample_kernel, seq=seq),
        grid=(1,),
        in_specs=[full, full, full, _resident(win.shape), _resident(cw.shape), _resident(wout.shape),
                  _resident((1, D_MODEL)), _resident((1, D_MODEL))],
        out_specs=[full, full],
        out_shape=[jax.ShapeDtypeStruct((n, D_MODEL), F32)] * 2,
        compiler_params=_params("arbitrary"),
        name="conv_sample",
    )(x, prev1, prev2, win, cw, wout, g, b)
    return y, u.reshape(nb, seq, D_MODEL)[:, seq - (CONV_W - 1):]


def _qkv_prompt_kernel(x_ref, w_ref, kf_ref, vf_ref, qt_ref, kh_ref, vt_ref, kmean_ref):
    xb = x_ref[...].astype(BF16)
    q = _dot(xb, w_ref[:, 0:D_MODEL])
    k = _dot(xb, w_ref[:, D_MODEL:2 * D_MODEL])
    v = _dot(xb, w_ref[:, 2 * D_MODEL:3 * D_MODEL])
    kf_ref[...] = k
    vf_ref[...] = v
    kmean_ref[0, 0] = jnp.sum(k, axis=0, keepdims=True) * (1.0 / BLOCK)
    for c in range(N_HEADS // HEADS_PER_STEP):
        lanes = slice(c * PAIR_LANES, (c + 1) * PAIR_LANES)
        qt = q[:, lanes].T
        vt = v[:, lanes].T
        kh_ref[0, c, 0] = k[:, lanes].astype(BF16)
        for s in range(HEADS_PER_STEP):
            hd = HEADS_PER_STEP * c + s
            qt_ref[0, hd] = qt[s * HEAD_DIM:(s + 1) * HEAD_DIM, :]
            vt_ref[0, hd, 0] = vt[s * HEAD_DIM:(s + 1) * HEAD_DIM, :].astype(BF16)


def _qkv_prompt(x, batch, w):
    n = x.shape[0]
    t = n // batch
    nb = t // BLOCK
    pairs = N_HEADS // HEADS_PER_STEP
    row = pl.BlockSpec((BLOCK, D_MODEL), lambda i: (i, 0))
    return pl.pallas_call(
        _qkv_prompt_kernel,
        grid=(n // BLOCK,),
        in_specs=[row, _resident(w.shape)],
        out_specs=[row, row,
                   pl.BlockSpec((1, N_HEADS, HEAD_DIM, BLOCK), lambda i: (i // nb, 0, 0, i % nb)),
                   pl.BlockSpec((1, pairs, 1, BLOCK, PAIR_LANES), lambda i: (i // nb, 0, i % nb, 0, 0)),
                   pl.BlockSpec((1, N_HEADS, 1, HEAD_DIM, BLOCK), lambda i: (i // nb, 0, i % nb, 0, 0)),
                   pl.BlockSpec((1, 1, 1, D_MODEL), lambda i: (i // nb, i % nb, 0, 0))],
        out_shape=[jax.ShapeDtypeStruct((n, D_MODEL), F32),
                   jax.ShapeDtypeStruct((n, D_MODEL), F32),
                   jax.ShapeDtypeStruct((batch, N_HEADS, HEAD_DIM, t), F32),
                   jax.ShapeDtypeStruct((batch, pairs, nb, BLOCK, PAIR_LANES), BF16),
                   jax.ShapeDtypeStruct((batch, N_HEADS, nb, HEAD_DIM, BLOCK), BF16),
                   jax.ShapeDtypeStruct((batch, nb, 1, D_MODEL), F32)],
        compiler_params=_params("parallel"),
        name="qkv_prompt",
    )(x, w)


def _matmul_kernel(x_ref, w_ref, y_ref):
    y_ref[...] = _dot(x_ref[...].astype(BF16), w_ref[...])


def _matmul(x, w):
    n, m = x.shape[0], w.shape[1]
    return pl.pallas_call(
        _matmul_kernel,
        grid=(1,),
        in_specs=[_resident(x.shape), _resident(w.shape)],
        out_specs=_resident((n, m)),
        out_shape=jax.ShapeDtypeStruct((n, m), F32),
        compiler_params=_params("arbitrary"),
        name="matmul",
    )(x, w)


def _select_topk(gate, blk, axis, n_blocks):
    sel = jnp.zeros(gate.shape, F32)
    for _ in range(TOP_K):
        mx = jnp.max(gate, axis=axis, keepdims=True)
        idx = jnp.min(jnp.where(gate == mx, blk, n_blocks), axis=axis, keepdims=True)
        hit = blk == idx
        sel = jnp.where(hit & (mx > -jnp.inf), 1.0, sel)
        gate = jnp.where(hit, -jnp.inf, gate)
    return sel


def _attn_prompt_kernel(slope_ref, qt_ref, k_ref, vt_ref, kmean_ref, ot_ref, sel_ref,
                        s_slots, p_slots):
    i = pl.program_id(2)
    nb = k_ref.shape[2]
    key = lax.broadcasted_iota(jnp.int32, (BLOCK, BLOCK), 0)
    qry = lax.broadcasted_iota(jnp.int32, (BLOCK, BLOCK), 1)
    rel = (qry - key).astype(F32)
    blk = lax.broadcasted_iota(jnp.int32, (nb, BLOCK), 0)
    zeros = jnp.zeros((HEAD_DIM, BLOCK), BF16)

    weights, biases = [], []
    for s in range(HEADS_PER_STEP):
        slope2 = slope_ref[s] * LOG2E
        qt = qt_ref[0, s]
        gate = jnp.dot(kmean_ref[0, s], qt, precision=lax.Precision.HIGHEST,
                       preferred_element_type=F32)
        gate = jnp.where(blk < i, gate, -jnp.inf)
        sel = _select_topk(gate, blk, 0, nb)
        shift = (i - blk).astype(F32) * (-BLOCK * slope2)
        sel_ref[s] = jnp.where(sel > 0.0, shift, NEG)
        qb = (qt * (HEAD_DIM ** -0.5 * LOG2E)).astype(BF16)
        pieces = [qb if r == s else zeros for r in range(HEADS_PER_STEP)]
        weights.append(jnp.concatenate(pieces, axis=0))
        biases.append(rel * (-slope2))

    def produce(j, slot, causal=False):
        kj = k_ref[0, 0, jnp.minimum(j, nb - 1)]
        maxima = []
        for s in range(HEADS_PER_STEP):
            t = _dot(kj, weights[s]) + biases[s]
            if causal:
                t = jnp.where(key <= qry, t, NEG)
            slot[s] = t
            maxima.append(jnp.max(t, axis=0, keepdims=True))
        return maxima

    def values(j, pslot):
        jj = jnp.minimum(j, nb - 1)
        return [_dot(vt_ref[0, s, jj], pslot[s]) for s in range(HEADS_PER_STEP)]

    def rows_of(j):
        return [sel_ref[s, pl.ds(j, 1), :] for s in range(HEADS_PER_STEP)]

    N_ST = 5

    def step(state, maxima, rows, k4, pv):
        out = []
        for s in range(HEADS_PER_STEP):
            m, l, acc, alpha_2, alpha_1 = state[N_ST * s:N_ST * s + N_ST]
            m_new = jnp.maximum(m, maxima[s] + rows[s])
            alpha = jnp.exp2(m - m_new)
            p = jnp.exp2(s_slots[k4, s] + (rows[s] - m_new))
            p_slots[k4, s] = p.astype(BF16)
            l = alpha * l + jnp.sum(p, axis=0, keepdims=True)
            if pv is not None:
                acc = alpha_2 * acc + pv[s]
            out += [m_new, l, acc, alpha_1, alpha]
        return out

    def block_of(k):
        return jnp.where(k <= 0, i, k - 1)

    one = jnp.ones((1, BLOCK), F32)
    state = []
    for s in range(HEADS_PER_STEP):
        state += [jnp.full((1, BLOCK), NEG, F32), jnp.zeros((1, BLOCK), F32),
                  jnp.zeros((HEAD_DIM, BLOCK), F32), one, one]
    p_slots[3] = jnp.zeros(p_slots.shape[1:], BF16)
    max_0 = produce(i, s_slots.at[0], causal=True)
    max_a = produce(0, s_slots.at[1])
    max_b = produce(1, s_slots.at[2])
    no_shift = [jnp.zeros((1, BLOCK), F32)] * HEADS_PER_STEP
    state = step(state, max_0, no_shift, 0, None)
    n_state = len(state)
    hps = HEADS_PER_STEP

    def body(g, carry):
        state = list(carry[:n_state])
        pending = [list(carry[n_state:n_state + hps]), list(carry[n_state + hps:])]
        k = TRIP_STAGES * g + 1
        for u in range(TRIP_STAGES):
            ahead = (u + 3) % 4
            pending.append(produce(k + u + 1, s_slots.at[ahead]))
            pv = values(block_of(k + u - 2), p_slots.at[ahead])
            state = step(state, pending[u], rows_of(k + u - 1), (u + 1) % 4, pv)
        return tuple(state) + tuple(pending[TRIP_STAGES]) + tuple(pending[TRIP_STAGES + 1])

    trips = (i + TRIP_STAGES - 1) // TRIP_STAGES
    carry = lax.fori_loop(0, trips, body, tuple(state) + tuple(max_a) + tuple(max_b))
    last = TRIP_STAGES * trips
    pv_2 = values(block_of(last - 1), p_slots.at[3])
    pv_1 = values(block_of(last), p_slots.at[0])
    for s in range(HEADS_PER_STEP):
        _, l, acc, alpha_2, alpha_1 = carry[N_ST * s:N_ST * s + N_ST]
        ot_ref[0, s] = (alpha_1 * (alpha_2 * acc + pv_2[s]) + pv_1[s]) / l


def _attn_prompt(slopes, qt, kh, vt, kmean):
    batch, _, _, t = qt.shape
    nb = t // BLOCK
    assert nb % TRIP_STAGES == 0
    hps = HEADS_PER_STEP
    slope_lanes = jnp.broadcast_to(slopes[:, None, None], (N_HEADS, 1, BLOCK))
    return pl.pallas_call(
        _attn_prompt_kernel,
        grid=(batch, N_HEADS // hps, nb),
        in_specs=[pl.BlockSpec((hps, 1, BLOCK), lambda b, h, i: (h, 0, 0)),
                  pl.BlockSpec((1, hps, HEAD_DIM, BLOCK), lambda b, h, i: (b, h, 0, i)),
                  pl.BlockSpec((1, 1, nb, BLOCK, PAIR_LANES), lambda b, h, i: (b, h, 0, 0, 0)),
                  pl.BlockSpec((1, hps, nb, HEAD_DIM, BLOCK), lambda b, h, i: (b, h, 0, 0, 0)),
                  pl.BlockSpec((1, hps, nb, HEAD_DIM), lambda b, h, i: (b, h, 0, 0))],
        out_specs=pl.BlockSpec((1, hps, HEAD_DIM, BLOCK), lambda b, h, i: (b, h, 0, i)),
        out_shape=jax.ShapeDtypeStruct((batch, N_HEADS, HEAD_DIM, t), F32),
        scratch_shapes=[pltpu.VMEM((hps, nb, BLOCK), F32),
                        pltpu.VMEM((4, hps, BLOCK, BLOCK), F32),
                        pltpu.VMEM((4, hps, BLOCK, BLOCK), BF16)],
        compiler_params=_params("parallel", "parallel", "arbitrary"),
        name="attn_prompt",
    )(slope_lanes, qt, kh, vt, kmean)


def _page_rows(page_ref):
    heads = pltpu.einshape("shd->hsd", page_ref[0, 0])
    return jnp.concatenate([heads[h] for h in range(N_HEADS)], axis=1)


def _page_spec(layer, n_pages, which):
    return pl.BlockSpec((1, 1, PAGE_SIZE, N_HEADS, HEAD_DIM),
                        lambda b, j, pt: (layer, pt[b * n_pages + PAGES_PER_BLOCK * j + which], 0, 0, 0))


def _head_rows(q4, seq):
    rows = jnp.concatenate([jnp.broadcast_to(q4[t:t + 1, :], (N_HEADS, D_MODEL)) for t in range(seq)], axis=0)
    r = lax.broadcasted_iota(jnp.int32, rows.shape, 0)
    lane = lax.broadcasted_iota(jnp.int32, rows.shape, 1)
    return jnp.where(r % N_HEADS == lane // HEAD_DIM, rows, 0.0)


def _sample_scores_kernel(pt_ref, k0_ref, k1_ref, q_ref, knew_ref, slope_ref, p_ref, pown_ref,
                          s_all, kmean_s, *, seq, n_blocks):
    j = pl.program_id(1)
    rows = seq * N_HEADS
    qbd = _head_rows(q_ref[0], seq)
    qb = (qbd * (HEAD_DIM ** -0.5)).astype(BF16)
    k0 = _page_rows(k0_ref)
    k1 = _page_rows(k1_ref)
    ksum = jnp.sum(k0, axis=0, keepdims=True) + jnp.sum(k1, axis=0, keepdims=True)
    kmean_s[pl.ds(j, 1), :] = ksum * (1.0 / BLOCK)
    s_all[j, :, 0:PAGE_SIZE] = _dot_nt(qb, k0.astype(BF16))
    s_all[j, :, PAGE_SIZE:BLOCK] = _dot_nt(qb, k1.astype(BF16))

    @pl.when(j == n_blocks - 1)
    def _():
        slope = slope_ref[...]
        gate = _dot_nt(qbd, kmean_s[...], precision=lax.Precision.HIGHEST)
        blk = lax.broadcasted_iota(jnp.int32, gate.shape, 1)
        sel = _select_topk(gate, blk, 1, n_blocks)
        past = n_blocks * BLOCK
        tok = lax.broadcasted_iota(jnp.int32, (rows, 1), 0) // N_HEADS
        lane = lax.broadcasted_iota(jnp.int32, (rows, BLOCK), 1)

        knew = knew_ref[0]
        own_lane = lax.broadcasted_iota(jnp.int32, (rows, PAGE_SIZE), 1)
        s_own = jnp.full((rows, PAGE_SIZE), NEG, F32)
        for s in range(seq):
            dots = jnp.sum(qbd * knew[s:s + 1, :], axis=1, keepdims=True) * (HEAD_DIM ** -0.5)
            logit = dots - slope * (tok - s).astype(F32)
            s_own = jnp.where((own_lane == s) & (tok >= s), logit, s_own)
        m = jnp.max(s_own, axis=1, keepdims=True)

        def logits(b):
            dist = (past + tok - b * BLOCK - lane).astype(F32)
            return jnp.where(sel[:, b:b + 1] > 0.0, s_all[b] - slope * dist, NEG)

        for b in range(n_blocks):
            m = jnp.maximum(m, jnp.max(logits(b), axis=1, keepdims=True))
        p_own = jnp.exp(s_own - m)
        l = jnp.sum(p_own, axis=1, keepdims=True)
        for b in range(n_blocks):
            l = l + jnp.sum(jnp.exp(logits(b) - m), axis=1, keepdims=True)
        inv = 1.0 / l
        pown_ref[0] = p_own * inv
        for b in range(n_blocks):
            p_ref[0, b] = (jnp.exp(logits(b) - m) * inv).astype(BF16)


def _sample_scores(page_table, cache, layer, q, knew, slope_rows):
    nbatch, seq, _ = q.shape
    n_pages = page_table.shape[1]
    n_blocks = n_pages // PAGES_PER_BLOCK
    rows = seq * N_HEADS
    grid_spec = pltpu.PrefetchScalarGridSpec(
        num_scalar_prefetch=1,
        grid=(nbatch, n_blocks),
        in_specs=[_page_spec(layer, n_pages, 0), _page_spec(layer, n_pages, 1),
                  pl.BlockSpec((1, seq, D_MODEL), lambda b, j, pt: (b, 0, 0)),
                  pl.BlockSpec((1, seq, D_MODEL), lambda b, j, pt: (b, 0, 0)),
                  pl.BlockSpec((rows, 1), lambda b, j, pt: (0, 0))],
        out_specs=[pl.BlockSpec((1, n_blocks, rows, BLOCK), lambda b, j, pt: (b, 0, 0, 0)),
                   pl.BlockSpec((1, rows, PAGE_SIZE), lambda b, j, pt: (b, 0, 0))],
        scratch_shapes=[pltpu.VMEM((n_blocks, rows, BLOCK), F32),
                        pltpu.VMEM((n_blocks, D_MODEL), F32)],
    )
    return pl.pallas_call(
        functools.partial(_sample_scores_kernel, seq=seq, n_blocks=n_blocks),
        grid_spec=grid_spec,
        out_shape=[jax.ShapeDtypeStruct((nbatch, n_blocks, rows, BLOCK), BF16),
                   jax.ShapeDtypeStruct((nbatch, rows, PAGE_SIZE), F32)],
        compiler_params=_params("parallel", "arbitrary"),
        name="sample_scores",
    )(page_table.reshape(-1), cache, cache, q, knew, slope_rows)


def _sample_values_kernel(pt_ref, v0_ref, v1_ref, p_ref, pown_ref, vnew_ref, o_ref, acc, *, seq, n_blocks):
    j = pl.program_id(1)

    @pl.when(j == 0)
    def _():
        acc[...] = jnp.zeros(acc.shape, F32)

    p = p_ref[0, 0]
    acc[...] += (_dot(p[:, 0:PAGE_SIZE], _page_rows(v0_ref).astype(BF16))
                 + _dot(p[:, PAGE_SIZE:BLOCK], _page_rows(v1_ref).astype(BF16)))

    @pl.when(j == n_blocks - 1)
    def _():
        o = acc[...]
        p_own = pown_ref[0]
        vnew = vnew_ref[0]
        for s in range(seq):
            o = o + p_own[:, s:s + 1] * vnew[s:s + 1, :]
        r = lax.broadcasted_iota(jnp.int32, o.shape, 0)
        lane = lax.broadcasted_iota(jnp.int32, o.shape, 1)
        o = jnp.where(r % N_HEADS == lane // HEAD_DIM, o, 0.0)
        o_ref[0] = jnp.sum(o.reshape(seq, N_HEADS, D_MODEL), axis=1)


def _sample_values(page_table, cache, layer, p, p_own, vnew):
    nbatch, n_blocks, rows, _ = p.shape
    seq = rows // N_HEADS
    n_pages = page_table.shape[1]
    grid_spec = pltpu.PrefetchScalarGridSpec(
        num_scalar_prefetch=1,
        grid=(nbatch, n_blocks),
        in_specs=[_page_spec(layer, n_pages, 0), _page_spec(layer, n_pages, 1),
                  pl.BlockSpec((1, 1, rows, BLOCK), lambda b, j, pt: (b, j, 0, 0)),
                  pl.BlockSpec((1, rows, PAGE_SIZE), lambda b, j, pt: (b, 0, 0)),
                  pl.BlockSpec((1, seq, D_MODEL), lambda b, j, pt: (b, 0, 0))],
        out_specs=pl.BlockSpec((1, seq, D_MODEL), lambda b, j, pt: (b, 0, 0)),
        scratch_shapes=[pltpu.VMEM((rows, D_MODEL), F32)],
    )
    return pl.pallas_call(
        functools.partial(_sample_values_kernel, seq=seq, n_blocks=n_blocks),
        grid_spec=grid_spec,
        out_shape=jax.ShapeDtypeStruct((nbatch, seq, D_MODEL), F32),
        compiler_params=_params("parallel", "arbitrary"),
        name="sample_values",
    )(page_table.reshape(-1), cache, cache, p, p_own, vnew)


def kernel(x_prompt, x_sample, cache_k, cache_v, state_conv, page_table, w_in_conv, conv_w, w_out_conv,
           w_qkv, w_o, w_gate_up, w_down, ln_g, ln_b):
    batch, seq_p, _ = x_prompt.shape
    nbatch, seq_s, _ = x_sample.shape
    assert seq_p % ROW_TILE == 0 and seq_p % BLOCK == 0 and seq_s >= CONV_W - 1
    assert (page_table.shape[1] * PAGE_SIZE) % BLOCK == 0

    slopes = 2.0 ** (-8.0 * jnp.arange(1, N_HEADS + 1, dtype=F32) / N_HEADS)
    slope_rows = jnp.tile(slopes, seq_s).reshape(seq_s * N_HEADS, 1)

    xp = x_prompt.reshape(batch * seq_p, D_MODEL)
    xs = x_sample.reshape(nbatch * seq_s, D_MODEL)
    k_p, v_p, k_s, v_s, conv_p, conv_s = [], [], [], [], [], []
    for i in range(DEPTH):
        li = i // 2
        g0, b0 = ln_g[i, 0:1], ln_b[i, 0:1]
        g1, b1 = ln_g[i, 1:2], ln_b[i, 1:2]
        if i % 2 == 0:
            win, wout = w_in_conv[li].astype(BF16), w_out_conv[li].astype(BF16)
            xp, cp = _conv_prompt(xp, batch, win, conv_w[li], wout, g0, b0)
            xs, cs = _conv_sample(xs, state_conv[li], seq_s, win, conv_w[li], wout, g0, b0)
            conv_p.append(cp)
            conv_s.append(cs)
        else:
            wqkv, wo = w_qkv[li].astype(BF16), w_o[li].astype(BF16)
            kf, vf, qt, kh, vt, kmean = _qkv_prompt(xp, batch, wqkv)
            kmean = kmean.reshape(batch, -1, N_HEADS, HEAD_DIM).transpose(0, 2, 1, 3)
            ot = _attn_prompt(slopes, qt, kh, vt, kmean)
            xp = _proj_ln(xp, ot.reshape(batch, D_MODEL, seq_p), wo, g0, b0)
            k_p.append(kf.reshape(batch, seq_p, N_HEADS, HEAD_DIM))
            v_p.append(vf.reshape(batch, seq_p, N_HEADS, HEAD_DIM))

            qkv = _matmul(xs, wqkv).reshape(nbatch, seq_s, 3, D_MODEL)
            q, kn, vn = qkv[:, :, 0], qkv[:, :, 1], qkv[:, :, 2]
            p, p_own = _sample_scores(page_table, cache_k, li, q, kn, slope_rows)
            o = _sample_values(page_table, cache_v, li, p, p_own, vn)
            xs = _proj_ln(xs, o.reshape(nbatch * seq_s, D_MODEL), wo, g0, b0)
            k_s.append(kn.reshape(nbatch, seq_s, N_HEADS, HEAD_DIM))
            v_s.append(vn.reshape(nbatch, seq_s, N_HEADS, HEAD_DIM))
        wgu, wd = w_gate_up[i].astype(BF16), w_down[i].astype(BF16)
        xp = _ffn_ln(xp, wgu, wd, g1, b1)
        xs = _ffn_ln(xs, wgu, wd, g1, b1)
    return (xp.reshape(batch, seq_p, D_MODEL), xs.reshape(nbatch, seq_s, D_MODEL),
            jnp.stack(k_p), jnp.stack(v_p), jnp.stack(conv_p),
            jnp.stack(k_s), jnp.stack(v_s), jnp.stack(conv_s))
```

```python
import functools

import jax
import jax.numpy as jnp
from jax import lax
from jax.experimental import pallas as pl
from jax.experimental.pallas import tpu as pltpu

D_MODEL = 1024
DEPTH = 4
N_HEADS = 16
HEAD_DIM = D_MODEL // N_HEADS
CONV_W = 3
BLOCK = 256
TOP_K = 3
PAGE_SIZE = 128
PAGES_PER_BLOCK = BLOCK // PAGE_SIZE
STEP_BLOCKS = 2
D_FF = -(-8 * D_MODEL // 768) * 256
ALPHA = (2 * DEPTH) ** 0.25
LN_EPS = 1e-5
NEG = -1e30
LOG2E = 1.4426950408889634
HEADS_PER_STEP = 2
PAIR_LANES = HEADS_PER_STEP * HEAD_DIM
TRIP_STAGES = 4

VMEM_LIMIT_BYTES = 56 * 1024 * 1024
ROW_TILE = 512
FF_CHUNKS = ((0, 768), (768, 1536), (1536, 2304), (2304, D_FF))

BF16 = jnp.bfloat16
F32 = jnp.float32


def _params(*sem):
    return pltpu.CompilerParams(dimension_semantics=sem, vmem_limit_bytes=VMEM_LIMIT_BYTES)


def _dot(a, b):
    return jnp.dot(a, b, preferred_element_type=F32)


def _dot_nt(a, b, precision=None):
    return lax.dot_general(a, b, (((1,), (1,)), ((), ())), precision=precision,
                           preferred_element_type=F32)


def _layer_norm(z, g, b):
    mu = jnp.mean(z, axis=-1, keepdims=True)
    zc = z - mu
    var = jnp.mean(zc * zc, axis=-1, keepdims=True)
    return zc * lax.rsqrt(var + LN_EPS) * g + b


def _resident(shape):
    zeros = (0,) * len(shape)
    return pl.BlockSpec(shape, lambda *_: zeros)


def _ffn_ln_kernel(x_ref, wgu_ref, wd_ref, g_ref, b_ref, y_ref):
    x = x_ref[...]
    xb = x.astype(BF16)
    acc = ALPHA * x
    for c0, c1 in FF_CHUNKS:
        gate = _dot(xb, wgu_ref[:, c0:c1])
        up = _dot(xb, wgu_ref[:, D_FF + c0:D_FF + c1])
        h = (gate * jax.nn.sigmoid(gate) * up).astype(BF16)
        acc = acc + _dot(h, wd_ref[c0:c1, :])
    y_ref[...] = _layer_norm(acc, g_ref[...], b_ref[...])


def _ffn_ln(x, wgu, wd, g, b):
    n = x.shape[0]
    tm = min(ROW_TILE, n)
    return pl.pallas_call(
        _ffn_ln_kernel,
        grid=(n // tm,),
        in_specs=[pl.BlockSpec((tm, D_MODEL), lambda i: (i, 0)),
                  _resident(wgu.shape), _resident(wd.shape),
                  _resident((1, D_MODEL)), _resident((1, D_MODEL))],
        out_specs=pl.BlockSpec((tm, D_MODEL), lambda i: (i, 0)),
        out_shape=jax.ShapeDtypeStruct((n, D_MODEL), F32),
        compiler_params=_params("parallel"),
        name="ffn_ln",
    )(x, wgu, wd, g, b)


def _proj_ln_kernel(x_ref, a_ref, w_ref, g_ref, b_ref, y_ref, *, a_transposed):
    a = a_ref[0].T if a_transposed else a_ref[...]
    z = ALPHA * x_ref[...] + _dot(a.astype(BF16), w_ref[...])
    y_ref[...] = _layer_norm(z, g_ref[...], b_ref[...])


def _proj_ln(x, a, w, g, b):
    n = x.shape[0]
    tm = min(ROW_TILE, n)
    if a.ndim == 3:
        tiles_per_seq = a.shape[2] // tm
        a_spec = pl.BlockSpec((1, D_MODEL, tm), lambda i: (i // tiles_per_seq, 0, i % tiles_per_seq))
    else:
        a_spec = pl.BlockSpec((tm, D_MODEL), lambda i: (i, 0))
    return pl.pallas_call(
        functools.partial(_proj_ln_kernel, a_transposed=a.ndim == 3),
        grid=(n // tm,),
        in_specs=[pl.BlockSpec((tm, D_MODEL), lambda i: (i, 0)), a_spec,
                  _resident(w.shape), _resident((1, D_MODEL)), _resident((1, D_MODEL))],
        out_specs=pl.BlockSpec((tm, D_MODEL), lambda i: (i, 0)),
        out_shape=jax.ShapeDtypeStruct((n, D_MODEL), F32),
        compiler_params=_params("parallel"),
        name="proj_ln",
    )(x, a, w, g, b)


CARRY = 8


def _conv_prompt_kernel(x_ref, win_ref, cw_ref, wout_ref, g_ref, b_ref, y_ref, state_ref, ubuf,
                        *, tm, tiles_per_seq):
    i = pl.program_id(0)
    x = x_ref[...]
    xb = x.astype(BF16)
    b_gate = _dot(xb, win_ref[:, 0:D_MODEL])
    c_gate = _dot(xb, win_ref[:, D_MODEL:2 * D_MODEL])
    h = _dot(xb, win_ref[:, 2 * D_MODEL:3 * D_MODEL])
    u = c_gate * h

    @pl.when(i % tiles_per_seq == 0)
    def _():
        ubuf[0:CARRY, :] = jnp.zeros((CARRY, D_MODEL), F32)

    @pl.when(i % tiles_per_seq != 0)
    def _():
        ubuf[0:CARRY, :] = ubuf[tm:tm + CARRY, :]

    ubuf[CARRY:CARRY + tm, :] = u
    cw = cw_ref[...]
    y = cw[2:3, :] * u
    for j in range(CONV_W - 1):
        off = CARRY - (CONV_W - 1) + j
        y = y + cw[j:j + 1, :] * ubuf[off:off + tm, :]
    m = _dot((b_gate * y).astype(BF16), wout_ref[...])
    y_ref[...] = _layer_norm(ALPHA * x + m, g_ref[...], b_ref[...])
    state_ref[0] = u[tm - (CONV_W - 1):tm, :]


def _conv_prompt(x, batch, win, cw, wout, g, b):
    n = x.shape[0]
    tm = ROW_TILE
    tiles_per_seq = n // batch // tm
    return pl.pallas_call(
        functools.partial(_conv_prompt_kernel, tm=tm, tiles_per_seq=tiles_per_seq),
        grid=(n // tm,),
        in_specs=[pl.BlockSpec((tm, D_MODEL), lambda i: (i, 0)),
                  _resident(win.shape), _resident(cw.shape), _resident(wout.shape),
                  _resident((1, D_MODEL)), _resident((1, D_MODEL))],
        out_specs=[pl.BlockSpec((tm, D_MODEL), lambda i: (i, 0)),
                   pl.BlockSpec((1, CONV_W - 1, D_MODEL), lambda i: (i // tiles_per_seq, 0, 0))],
        out_shape=[jax.ShapeDtypeStruct((n, D_MODEL), F32),
                   jax.ShapeDtypeStruct((batch, CONV_W - 1, D_MODEL), F32)],
        scratch_shapes=[pltpu.VMEM((tm + CARRY, D_MODEL), F32)],
        compiler_params=_params("arbitrary"),
        name="conv_prompt",
    )(x, win, cw, wout, g, b)


def _conv_sample_kernel(x_ref, p1_ref, p2_ref, win_ref, cw_ref, wout_ref, g_ref, b_ref,
                        y_ref, u_ref, *, seq):
    x = x_ref[...]
    n = x.shape[0]
    xb = x.astype(BF16)
    b_gate = _dot(xb, win_ref[:, 0:D_MODEL])
    c_gate = _dot(xb, win_ref[:, D_MODEL:2 * D_MODEL])
    h = _dot(xb, win_ref[:, 2 * D_MODEL:3 * D_MODEL])
    u = c_gate * h
    pos = lax.broadcasted_iota(jnp.int32, (n, D_MODEL), 0) % seq
    u1 = jnp.where(pos >= 1, pltpu.roll(u, 1, axis=0), p1_ref[...])
    u2 = jnp.where(pos >= 2, pltpu.roll(u, 2, axis=0), p2_ref[...])
    cw = cw_ref[...]
    y = cw[0:1, :] * u2 + cw[1:2, :] * u1 + cw[2:3, :] * u
    m = _dot((b_gate * y).astype(BF16), wout_ref[...])
    y_ref[...] = _layer_norm(ALPHA * x + m, g_ref[...], b_ref[...])
    u_ref[...] = u


def _conv_sample(x, state, seq, win, cw, wout, g, b):
    n = x.shape[0]
    nb = n // seq
    zeros = jnp.zeros((nb, seq, D_MODEL), F32)
    prev1 = zeros.at[:, 0].set(state[:, 1]).reshape(n, D_MODEL)
    prev2 = zeros.at[:, 0].set(state[:, 0]).at[:, 1].set(state[:, 1]).reshape(n, D_MODEL)
    full = _resident((n, D_MODEL))
    y, u = pl.pallas_call(
        functools.partial(_conv_sample_kernel, seq=seq),
        grid=(1,),
        in_specs=[full, full, full, _resident(win.shape), _resident(cw.shape), _resident(wout.shape),
                  _resident((1, D_MODEL)), _resident((1, D_MODEL))],
        out_specs=[full, full],
        out_shape=[jax.ShapeDtypeStruct((n, D_MODEL), F32)] * 2,
        compiler_params=_params("arbitrary"),
        name="conv_sample",
    )(x, prev1, prev2, win, cw, wout, g, b)
    return y, u.reshape(nb, seq, D_MODEL)[:, seq - (CONV_W - 1):]


def _qkv_prompt_kernel(x_ref, w_ref, kf_ref, vf_ref, qt_ref, kh_ref, vt_ref, kmean_ref):
    xb = x_ref[...].astype(BF16)
    q = _dot(xb, w_ref[:, 0:D_MODEL])
    k = _dot(xb, w_ref[:, D_MODEL:2 * D_MODEL])
    v = _dot(xb, w_ref[:, 2 * D_MODEL:3 * D_MODEL])
    kf_ref[...] = k
    vf_ref[...] = v
    kmean_ref[0, 0] = jnp.sum(k, axis=0, keepdims=True) * (1.0 / BLOCK)
    for c in range(N_HEADS // HEADS_PER_STEP):
        lanes = slice(c * PAIR_LANES, (c + 1) * PAIR_LANES)
        qt = q[:, lanes].T
        vt = v[:, lanes].T
        kh_ref[0, c, 0] = k[:, lanes].astype(BF16)
        for s in range(HEADS_PER_STEP):
            hd = HEADS_PER_STEP * c + s
            qt_ref[0, hd] = qt[s * HEAD_DIM:(s + 1) * HEAD_DIM, :]
            vt_ref[0, hd, 0] = vt[s * HEAD_DIM:(s + 1) * HEAD_DIM, :].astype(BF16)


def _qkv_prompt(x, batch, w):
    n = x.shape[0]
    t = n // batch
    nb = t // BLOCK
    pairs = N_HEADS // HEADS_PER_STEP
    row = pl.BlockSpec((BLOCK, D_MODEL), lambda i: (i, 0))
    return pl.pallas_call(
        _qkv_prompt_kernel,
        grid=(n // BLOCK,),
        in_specs=[row, _resident(w.shape)],
        out_specs=[row, row,
                   pl.BlockSpec((1, N_HEADS, HEAD_DIM, BLOCK), lambda i: (i // nb, 0, 0, i % nb)),
                   pl.BlockSpec((1, pairs, 1, BLOCK, PAIR_LANES), lambda i: (i // nb, 0, i % nb, 0, 0)),
                   pl.BlockSpec((1, N_HEADS, 1, HEAD_DIM, BLOCK), lambda i: (i // nb, 0, i % nb, 0, 0)),
                   pl.BlockSpec((1, 1, 1, D_MODEL), lambda i: (i // nb, i % nb, 0, 0))],
        out_shape=[jax.ShapeDtypeStruct((n, D_MODEL), F32),
                   jax.ShapeDtypeStruct((n, D_MODEL), F32),
                   jax.ShapeDtypeStruct((batch, N_HEADS, HEAD_DIM, t), F32),
                   jax.ShapeDtypeStruct((batch, pairs, nb, BLOCK, PAIR_LANES), BF16),
                   jax.ShapeDtypeStruct((batch, N_HEADS, nb, HEAD_DIM, BLOCK), BF16),
                   jax.ShapeDtypeStruct((batch, nb, 1, D_MODEL), F32)],
        compiler_params=_params("parallel"),
        name="qkv_prompt",
    )(x, w)


def _matmul_kernel(x_ref, w_ref, y_ref):
    y_ref[...] = _dot(x_ref[...].astype(BF16), w_ref[...])


def _matmul(x, w):
    n, m = x.shape[0], w.shape[1]
    return pl.pallas_call(
        _matmul_kernel,
        grid=(1,),
        in_specs=[_resident(x.shape), _resident(w.shape)],
        out_specs=_resident((n, m)),
        out_shape=jax.ShapeDtypeStruct((n, m), F32),
        compiler_params=_params("arbitrary"),
        name="matmul",
    )(x, w)


def _select_topk(gate, blk, axis, n_blocks):
    sel = jnp.zeros(gate.shape, F32)
    for _ in range(TOP_K):
        mx = jnp.max(gate, axis=axis, keepdims=True)
        idx = jnp.min(jnp.where(gate == mx, blk, n_blocks), axis=axis, keepdims=True)
        hit = blk == idx
        sel = jnp.where(hit & (mx > -jnp.inf), 1.0, sel)
        gate = jnp.where(hit, -jnp.inf, gate)
    return sel


def _attn_prompt_kernel(slope_ref, qt_ref, k_ref, vt_ref, kmean_ref, ot_ref, sel_ref,
                        s_slots, p_slots):
    i = pl.program_id(2)
    nb = k_ref.shape[2]
    key = lax.broadcasted_iota(jnp.int32, (BLOCK, BLOCK), 0)
    qry = lax.broadcasted_iota(jnp.int32, (BLOCK, BLOCK), 1)
    rel = (qry - key).astype(F32)
    blk = lax.broadcasted_iota(jnp.int32, (nb, BLOCK), 0)
    zeros = jnp.zeros((HEAD_DIM, BLOCK), BF16)

    weights, biases = [], []
    for s in range(HEADS_PER_STEP):
        slope2 = slope_ref[s] * LOG2E
        qt = qt_ref[0, s]
        gate = jnp.dot(kmean_ref[0, s], qt, precision=lax.Precision.HIGHEST,
                       preferred_element_type=F32)
        gate = jnp.where(blk < i, gate, -jnp.inf)
        sel = _select_topk(gate, blk, 0, nb)
        shift = (i - blk).astype(F32) * (-BLOCK * slope2)
        sel_ref[s] = jnp.where(sel > 0.0, shift, NEG)
        qb = (qt * (HEAD_DIM ** -0.5 * LOG2E)).astype(BF16)
        pieces = [qb if r == s else zeros for r in range(HEADS_PER_STEP)]
        weights.append(jnp.concatenate(pieces, axis=0))
        biases.append(rel * (-slope2))

    def produce(j, slot, causal=False):
        kj = k_ref[0, 0, jnp.minimum(j, nb - 1)]
        maxima = []
        for s in range(HEADS_PER_STEP):
            t = _dot(kj, weights[s]) + biases[s]
            if causal:
                t = jnp.where(key <= qry, t, NEG)
            slot[s] = t
            maxima.append(jnp.max(t, axis=0, keepdims=True))
        return maxima

    def values(j, pslot):
        jj = jnp.minimum(j, nb - 1)
        return [_dot(vt_ref[0, s, jj], pslot[s]) for s in range(HEADS_PER_STEP)]

    def rows_of(j):
        return [sel_ref[s, pl.ds(j, 1), :] for s in range(HEADS_PER_STEP)]

    N_ST = 5

    def step(state, maxima, rows, k4, pv):
        out = []
        for s in range(HEADS_PER_STEP):
            m, l, acc, alpha_2, alpha_1 = state[N_ST * s:N_ST * s + N_ST]
            m_new = jnp.maximum(m, maxima[s] + rows[s])
            alpha = jnp.exp2(m - m_new)
            p = jnp.exp2(s_slots[k4, s] + (rows[s] - m_new))
            p_slots[k4, s] = p.astype(BF16)
            l = alpha * l + jnp.sum(p, axis=0, keepdims=True)
            if pv is not None:
                acc = alpha_2 * acc + pv[s]
            out += [m_new, l, acc, alpha_1, alpha]
        return out

    def block_of(k):
        return jnp.where(k <= 0, i, k - 1)

    one = jnp.ones((1, BLOCK), F32)
    state = []
    for s in range(HEADS_PER_STEP):
        state += [jnp.full((1, BLOCK), NEG, F32), jnp.zeros((1, BLOCK), F32),
                  jnp.zeros((HEAD_DIM, BLOCK), F32), one, one]
    p_slots[3] = jnp.zeros(p_slots.shape[1:], BF16)
    max_0 = produce(i, s_slots.at[0], causal=True)
    max_a = produce(0, s_slots.at[1])
    max_b = produce(1, s_slots.at[2])
    no_shift = [jnp.zeros((1, BLOCK), F32)] * HEADS_PER_STEP
    state = step(state, max_0, no_shift, 0, None)
    n_state = len(state)
    hps = HEADS_PER_STEP

    def body(g, carry):
        state = list(carry[:n_state])
        pending = [list(carry[n_state:n_state + hps]), list(carry[n_state + hps:])]
        k = TRIP_STAGES * g + 1
        for u in range(TRIP_STAGES):
            ahead = (u + 3) % 4
            pending.append(produce(k + u + 1, s_slots.at[ahead]))
            pv = values(block_of(k + u - 2), p_slots.at[ahead])
            state = step(state, pending[u], rows_of(k + u - 1), (u + 1) % 4, pv)
        return tuple(state) + tuple(pending[TRIP_STAGES]) + tuple(pending[TRIP_STAGES + 1])

    trips = (i + TRIP_STAGES - 1) // TRIP_STAGES
    carry = lax.fori_loop(0, trips, body, tuple(state) + tuple(max_a) + tuple(max_b))
    last = TRIP_STAGES * trips
    pv_2 = values(block_of(last - 1), p_slots.at[3])
    pv_1 = values(block_of(last), p_slots.at[0])
    for s in range(HEADS_PER_STEP):
        _, l, acc, alpha_2, alpha_1 = carry[N_ST * s:N_ST * s + N_ST]
        ot_ref[0, s] = (alpha_1 * (alpha_2 * acc + pv_2[s]) + pv_1[s]) / l


def _attn_prompt(slopes, qt, kh, vt, kmean):
    batch, _, _, t = qt.shape
    nb = t // BLOCK
    assert nb % TRIP_STAGES == 0
    hps = HEADS_PER_STEP
    slope_lanes = jnp.broadcast_to(slopes[:, None, None], (N_HEADS, 1, BLOCK))
    return pl.pallas_call(
        _attn_prompt_kernel,
        grid=(batch, N_HEADS // hps, nb),
        in_specs=[pl.BlockSpec((hps, 1, BLOCK), lambda b, h, i: (h, 0, 0)),
                  pl.BlockSpec((1, hps, HEAD_DIM, BLOCK), lambda b, h, i: (b, h, 0, i)),
                  pl.BlockSpec((1, 1, nb, BLOCK, PAIR_LANES), lambda b, h, i: (b, h, 0, 0, 0)),
                  pl.BlockSpec((1, hps, nb, HEAD_DIM, BLOCK), lambda b, h, i: (b, h, 0, 0, 0)),
                  pl.BlockSpec((1, hps, nb, HEAD_DIM), lambda b, h, i: (b, h, 0, 0))],
        out_specs=pl.BlockSpec((1, hps, HEAD_DIM, BLOCK), lambda b, h, i: (b, h, 0, i)),
        out_shape=jax.ShapeDtypeStruct((batch, N_HEADS, HEAD_DIM, t), F32),
        scratch_shapes=[pltpu.VMEM((hps, nb, BLOCK), F32),
                        pltpu.VMEM((4, hps, BLOCK, BLOCK), F32),
                        pltpu.VMEM((4, hps, BLOCK, BLOCK), BF16)],
        compiler_params=_params("parallel", "parallel", "arbitrary"),
        name="attn_prompt",
    )(slope_lanes, qt, kh, vt, kmean)


def _page_matrix(page_ref):
    return page_ref[0, 0].reshape(D_MODEL, PAGE_SIZE)


def _page_specs(layer, n_pages):
    per_step = STEP_BLOCKS * PAGES_PER_BLOCK

    def spec(which):
        return pl.BlockSpec((1, 1, N_HEADS, HEAD_DIM, PAGE_SIZE),
                            lambda b, j, pt: (layer, pt[b * n_pages + per_step * j + which], 0, 0, 0))

    return [spec(w) for w in range(per_step)]


def _head_rows(q4, seq):
    rows = jnp.concatenate([jnp.broadcast_to(q4[t:t + 1, :], (N_HEADS, D_MODEL)) for t in range(seq)], axis=0)
    r = lax.broadcasted_iota(jnp.int32, rows.shape, 0)
    lane = lax.broadcasted_iota(jnp.int32, rows.shape, 1)
    return jnp.where(r % N_HEADS == lane // HEAD_DIM, rows, 0.0)


def _sample_scores_kernel(pt_ref, *refs, seq, n_blocks):
    n_pages = STEP_BLOCKS * PAGES_PER_BLOCK
    page_refs = refs[:n_pages]
    q_ref, knew_ref, slope_ref, p_ref, pown_ref, s_all, kmean_t = refs[n_pages:]
    j = pl.program_id(1)
    rows = seq * N_HEADS
    qbd = _head_rows(q_ref[0], seq)
    qb = (qbd * (HEAD_DIM ** -0.5)).astype(BF16)

    @pl.when(j == 0)
    def _():
        kmean_t[...] = jnp.zeros(kmean_t.shape, F32)

    lane = lax.broadcasted_iota(jnp.int32, kmean_t.shape, 1)
    for u in range(STEP_BLOCKS):
        blk = j * STEP_BLOCKS + u
        ksum = jnp.zeros((D_MODEL, 1), F32)
        for w in range(PAGES_PER_BLOCK):
            kp = _page_matrix(page_refs[u * PAGES_PER_BLOCK + w])
            ksum = ksum + jnp.sum(kp, axis=1, keepdims=True)
            s_all[blk, :, w * PAGE_SIZE:(w + 1) * PAGE_SIZE] = _dot(qb, kp.astype(BF16))
        kmean_t[...] = jnp.where(lane == blk, ksum * (1.0 / BLOCK), kmean_t[...])

    @pl.when(j == n_blocks // STEP_BLOCKS - 1)
    def _():
        slope = slope_ref[...]
        gate = jnp.dot(qbd, kmean_t[...], precision=lax.Precision.HIGHEST,
                       preferred_element_type=F32)
        blk = lax.broadcasted_iota(jnp.int32, gate.shape, 1)
        gate = jnp.where(blk < n_blocks, gate, -jnp.inf)
        sel = _select_topk(gate, blk, 1, n_blocks)
        past = n_blocks * BLOCK
        tok = lax.broadcasted_iota(jnp.int32, (rows, 1), 0) // N_HEADS
        lane = lax.broadcasted_iota(jnp.int32, (rows, BLOCK), 1)

        knew = knew_ref[0]
        own_lane = lax.broadcasted_iota(jnp.int32, (rows, PAGE_SIZE), 1)
        s_own = jnp.full((rows, PAGE_SIZE), NEG, F32)
        for s in range(seq):
            dots = jnp.sum(qbd * knew[s:s + 1, :], axis=1, keepdims=True) * (HEAD_DIM ** -0.5)
            logit = dots - slope * (tok - s).astype(F32)
            s_own = jnp.where((own_lane == s) & (tok >= s), logit, s_own)
        m = jnp.max(s_own, axis=1, keepdims=True)

        def logits(b):
            dist = (past + tok - b * BLOCK - lane).astype(F32)
            return jnp.where(sel[:, b:b + 1] > 0.0, s_all[b] - slope * dist, NEG)

        for b in range(n_blocks):
            m = jnp.maximum(m, jnp.max(logits(b), axis=1, keepdims=True))
        p_own = jnp.exp(s_own - m)
        l = jnp.sum(p_own, axis=1, keepdims=True)
        for b in range(n_blocks):
            l = l + jnp.sum(jnp.exp(logits(b) - m), axis=1, keepdims=True)
        inv = 1.0 / l
        pown_ref[0] = p_own * inv
        for b in range(n_blocks):
            p_ref[0, b] = (jnp.exp(logits(b) - m) * inv).astype(BF16)


def _sample_scores(page_table, cache, layer, q, knew, slope_rows):
    nbatch, seq, _ = q.shape
    n_pages = page_table.shape[1]
    n_blocks = n_pages // PAGES_PER_BLOCK
    rows = seq * N_HEADS
    grid_spec = pltpu.PrefetchScalarGridSpec(
        num_scalar_prefetch=1,
        grid=(nbatch, n_blocks // STEP_BLOCKS),
        in_specs=[*_page_specs(layer, n_pages),
                  pl.BlockSpec((1, seq, D_MODEL), lambda b, j, pt: (b, 0, 0)),
                  pl.BlockSpec((1, seq, D_MODEL), lambda b, j, pt: (b, 0, 0)),
                  pl.BlockSpec((rows, 1), lambda b, j, pt: (0, 0))],
        out_specs=[pl.BlockSpec((1, n_blocks, rows, BLOCK), lambda b, j, pt: (b, 0, 0, 0)),
                   pl.BlockSpec((1, rows, PAGE_SIZE), lambda b, j, pt: (b, 0, 0))],
        scratch_shapes=[pltpu.VMEM((n_blocks, rows, BLOCK), F32),
                        pltpu.VMEM((D_MODEL, max(n_blocks, PAGE_SIZE)), F32)],
    )
    return pl.pallas_call(
        functools.partial(_sample_scores_kernel, seq=seq, n_blocks=n_blocks),
        grid_spec=grid_spec,
        out_shape=[jax.ShapeDtypeStruct((nbatch, n_blocks, rows, BLOCK), BF16),
                   jax.ShapeDtypeStruct((nbatch, rows, PAGE_SIZE), F32)],
        compiler_params=_params("parallel", "arbitrary"),
        name="sample_scores",
    )(page_table.reshape(-1), *[cache] * (STEP_BLOCKS * PAGES_PER_BLOCK), q, knew, slope_rows)


def _sample_values_kernel(pt_ref, *refs, seq, n_blocks):
    n_pages = STEP_BLOCKS * PAGES_PER_BLOCK
    page_refs = refs[:n_pages]
    p_ref, pown_ref, vnew_ref, o_ref, acc = refs[n_pages:]
    j = pl.program_id(1)

    @pl.when(j == 0)
    def _():
        acc[...] = jnp.zeros(acc.shape, F32)

    total = jnp.zeros(acc.shape, F32)
    for u in range(STEP_BLOCKS):
        p = p_ref[0, u]
        for w in range(PAGES_PER_BLOCK):
            vp = _page_matrix(page_refs[u * PAGES_PER_BLOCK + w]).astype(BF16)
            total = total + _dot_nt(p[:, w * PAGE_SIZE:(w + 1) * PAGE_SIZE], vp)
    acc[...] += total

    @pl.when(j == n_blocks // STEP_BLOCKS - 1)
    def _():
        o = acc[...]
        p_own = pown_ref[0]
        vnew = vnew_ref[0]
        for s in range(seq):
            o = o + p_own[:, s:s + 1] * vnew[s:s + 1, :]
        r = lax.broadcasted_iota(jnp.int32, o.shape, 0)
        lane = lax.broadcasted_iota(jnp.int32, o.shape, 1)
        o = jnp.where(r % N_HEADS == lane // HEAD_DIM, o, 0.0)
        o_ref[0] = jnp.sum(o.reshape(seq, N_HEADS, D_MODEL), axis=1)


def _sample_values(page_table, cache, layer, p, p_own, vnew):
    nbatch, n_blocks, rows, _ = p.shape
    seq = rows // N_HEADS
    n_pages = page_table.shape[1]
    grid_spec = pltpu.PrefetchScalarGridSpec(
        num_scalar_prefetch=1,
        grid=(nbatch, n_blocks // STEP_BLOCKS),
        in_specs=[*_page_specs(layer, n_pages),
                  pl.BlockSpec((1, STEP_BLOCKS, rows, BLOCK), lambda b, j, pt: (b, j, 0, 0)),
                  pl.BlockSpec((1, rows, PAGE_SIZE), lambda b, j, pt: (b, 0, 0)),
                  pl.BlockSpec((1, seq, D_MODEL), lambda b, j, pt: (b, 0, 0))],
        out_specs=pl.BlockSpec((1, seq, D_MODEL), lambda b, j, pt: (b, 0, 0)),
        scratch_shapes=[pltpu.VMEM((rows, D_MODEL), F32)],
    )
    return pl.pallas_call(
        functools.partial(_sample_values_kernel, seq=seq, n_blocks=n_blocks),
        grid_spec=grid_spec,
        out_shape=jax.ShapeDtypeStruct((nbatch, seq, D_MODEL), F32),
        compiler_params=_params("parallel", "arbitrary"),
        name="sample_values",
    )(page_table.reshape(-1), *[cache] * (STEP_BLOCKS * PAGES_PER_BLOCK), p, p_own, vnew)


def kernel(x_prompt, x_sample, cache_k, cache_v, state_conv, page_table, w_in_conv, conv_w, w_out_conv,
           w_qkv, w_o, w_gate_up, w_down, ln_g, ln_b):
    batch, seq_p, _ = x_prompt.shape
    nbatch, seq_s, _ = x_sample.shape
    assert seq_p % ROW_TILE == 0 and seq_p % BLOCK == 0 and seq_s >= CONV_W - 1
    assert page_table.shape[1] % (STEP_BLOCKS * PAGES_PER_BLOCK) == 0

    slopes = 2.0 ** (-8.0 * jnp.arange(1, N_HEADS + 1, dtype=F32) / N_HEADS)
    slope_rows = jnp.tile(slopes, seq_s).reshape(seq_s * N_HEADS, 1)

    cache_k = cache_k.transpose(0, 1, 3, 4, 2)
    cache_v = cache_v.transpose(0, 1, 3, 4, 2)

    xp = x_prompt.reshape(batch * seq_p, D_MODEL)
    xs = x_sample.reshape(nbatch * seq_s, D_MODEL)
    k_p, v_p, k_s, v_s, conv_p, conv_s = [], [], [], [], [], []
    for i in range(DEPTH):
        li = i // 2
        g0, b0 = ln_g[i, 0:1], ln_b[i, 0:1]
        g1, b1 = ln_g[i, 1:2], ln_b[i, 1:2]
        if i % 2 == 0:
            win, wout = w_in_conv[li].astype(BF16), w_out_conv[li].astype(BF16)
            xp, cp = _conv_prompt(xp, batch, win, conv_w[li], wout, g0, b0)
            xs, cs = _conv_sample(xs, state_conv[li], seq_s, win, conv_w[li], wout, g0, b0)
            conv_p.append(cp)
            conv_s.append(cs)
        else:
            wqkv, wo = w_qkv[li].astype(BF16), w_o[li].astype(BF16)
            kf, vf, qt, kh, vt, kmean = _qkv_prompt(xp, batch, wqkv)
            kmean = kmean.reshape(batch, -1, N_HEADS, HEAD_DIM).transpose(0, 2, 1, 3)
            ot = _attn_prompt(slopes, qt, kh, vt, kmean)
            xp = _proj_ln(xp, ot.reshape(batch, D_MODEL, seq_p), wo, g0, b0)
            k_p.append(kf.reshape(batch, seq_p, N_HEADS, HEAD_DIM))
            v_p.append(vf.reshape(batch, seq_p, N_HEADS, HEAD_DIM))

            qkv = _matmul(xs, wqkv).reshape(nbatch, seq_s, 3, D_MODEL)
            q, kn, vn = qkv[:, :, 0], qkv[:, :, 1], qkv[:, :, 2]
            p, p_own = _sample_scores(page_table, cache_k, li, q, kn, slope_rows)
            o = _sample_values(page_table, cache_v, li, p, p_own, vn)
            xs = _proj_ln(xs, o.reshape(nbatch * seq_s, D_MODEL), wo, g0, b0)
            k_s.append(kn.reshape(nbatch, seq_s, N_HEADS, HEAD_DIM))
            v_s.append(vn.reshape(nbatch, seq_s, N_HEADS, HEAD_DIM))
        wgu, wd = w_gate_up[i].astype(BF16), w_down[i].astype(BF16)
        xp = _ffn_ln(xp, wgu, wd, g1, b1)
        xs = _ffn_ln(xs, wgu, wd, g1, b1)
    return (xp.reshape(batch, seq_p, D_MODEL), xs.reshape(nbatch, seq_s, D_MODEL),
            jnp.stack(k_p), jnp.stack(v_p), jnp.stack(conv_p),
            jnp.stack(k_s), jnp.stack(v_s), jnp.stack(conv_s))
```

```python
import functools

import jax
import jax.numpy as jnp
from jax import lax
from jax.experimental import pallas as pl
from jax.experimental.pallas import tpu as pltpu

D_MODEL = 1024
DEPTH = 4
N_HEADS = 16
HEAD_DIM = D_MODEL // N_HEADS
CONV_W = 3
BLOCK = 256
TOP_K = 3
PAGE_SIZE = 128
PAGES_PER_BLOCK = BLOCK // PAGE_SIZE
STEP_BLOCKS = 4
D_FF = -(-8 * D_MODEL // 768) * 256
ALPHA = (2 * DEPTH) ** 0.25
LN_EPS = 1e-5
NEG = -1e30
LOG2E = 1.4426950408889634
HEADS_PER_STEP = 2
PAIR_LANES = HEADS_PER_STEP * HEAD_DIM
TRIP_STAGES = 4

VMEM_LIMIT_BYTES = 56 * 1024 * 1024
ROW_TILE = 512
FF_CHUNKS = ((0, 768), (768, 1536), (1536, 2304), (2304, D_FF))

BF16 = jnp.bfloat16
F32 = jnp.float32


def _params(*sem):
    return pltpu.CompilerParams(dimension_semantics=sem, vmem_limit_bytes=VMEM_LIMIT_BYTES)


def _dot(a, b):
    return jnp.dot(a, b, preferred_element_type=F32)


def _dot_nt(a, b, precision=None):
    return lax.dot_general(a, b, (((1,), (1,)), ((), ())), precision=precision,
                           preferred_element_type=F32)


def _layer_norm(z, g, b):
    mu = jnp.mean(z, axis=-1, keepdims=True)
    zc = z - mu
    var = jnp.mean(zc * zc, axis=-1, keepdims=True)
    return zc * lax.rsqrt(var + LN_EPS) * g + b


def _resident(shape):
    zeros = (0,) * len(shape)
    return pl.BlockSpec(shape, lambda *_: zeros)


def _ffn_ln_kernel(x_ref, wgu_ref, wd_ref, g_ref, b_ref, y_ref):
    x = x_ref[...]
    xb = x.astype(BF16)
    acc = ALPHA * x
    for c0, c1 in FF_CHUNKS:
        gate = _dot(xb, wgu_ref[:, c0:c1])
        up = _dot(xb, wgu_ref[:, D_FF + c0:D_FF + c1])
        h = (gate * jax.nn.sigmoid(gate) * up).astype(BF16)
        acc = acc + _dot(h, wd_ref[c0:c1, :])
    y_ref[...] = _layer_norm(acc, g_ref[...], b_ref[...])


def _ffn_ln(x, wgu, wd, g, b):
    n = x.shape[0]
    tm = min(ROW_TILE, n)
    return pl.pallas_call(
        _ffn_ln_kernel,
        grid=(n // tm,),
        in_specs=[pl.BlockSpec((tm, D_MODEL), lambda i: (i, 0)),
                  _resident(wgu.shape), _resident(wd.shape),
                  _resident((1, D_MODEL)), _resident((1, D_MODEL))],
        out_specs=pl.BlockSpec((tm, D_MODEL), lambda i: (i, 0)),
        out_shape=jax.ShapeDtypeStruct((n, D_MODEL), F32),
        compiler_params=_params("parallel"),
        name="ffn_ln",
    )(x, wgu, wd, g, b)


def _proj_ln_kernel(x_ref, a_ref, w_ref, g_ref, b_ref, y_ref, *, a_transposed):
    a = a_ref[0].T if a_transposed else a_ref[...]
    z = ALPHA * x_ref[...] + _dot(a.astype(BF16), w_ref[...])
    y_ref[...] = _layer_norm(z, g_ref[...], b_ref[...])


def _proj_ln(x, a, w, g, b):
    n = x.shape[0]
    tm = min(ROW_TILE, n)
    if a.ndim == 3:
        tiles_per_seq = a.shape[2] // tm
        a_spec = pl.BlockSpec((1, D_MODEL, tm), lambda i: (i // tiles_per_seq, 0, i % tiles_per_seq))
    else:
        a_spec = pl.BlockSpec((tm, D_MODEL), lambda i: (i, 0))
    return pl.pallas_call(
        functools.partial(_proj_ln_kernel, a_transposed=a.ndim == 3),
        grid=(n // tm,),
        in_specs=[pl.BlockSpec((tm, D_MODEL), lambda i: (i, 0)), a_spec,
                  _resident(w.shape), _resident((1, D_MODEL)), _resident((1, D_MODEL))],
        out_specs=pl.BlockSpec((tm, D_MODEL), lambda i: (i, 0)),
        out_shape=jax.ShapeDtypeStruct((n, D_MODEL), F32),
        compiler_params=_params("parallel"),
        name="proj_ln",
    )(x, a, w, g, b)


CARRY = 8


def _conv_prompt_kernel(x_ref, win_ref, cw_ref, wout_ref, g_ref, b_ref, y_ref, state_ref, ubuf,
                        *, tm, tiles_per_seq):
    i = pl.program_id(0)
    x = x_ref[...]
    xb = x.astype(BF16)
    b_gate = _dot(xb, win_ref[:, 0:D_MODEL])
    c_gate = _dot(xb, win_ref[:, D_MODEL:2 * D_MODEL])
    h = _dot(xb, win_ref[:, 2 * D_MODEL:3 * D_MODEL])
    u = c_gate * h

    @pl.when(i % tiles_per_seq == 0)
    def _():
        ubuf[0:CARRY, :] = jnp.zeros((CARRY, D_MODEL), F32)

    @pl.when(i % tiles_per_seq != 0)
    def _():
        ubuf[0:CARRY, :] = ubuf[tm:tm + CARRY, :]

    ubuf[CARRY:CARRY + tm, :] = u
    cw = cw_ref[...]
    y = cw[2:3, :] * u
    for j in range(CONV_W - 1):
        off = CARRY - (CONV_W - 1) + j
        y = y + cw[j:j + 1, :] * ubuf[off:off + tm, :]
    m = _dot((b_gate * y).astype(BF16), wout_ref[...])
    y_ref[...] = _layer_norm(ALPHA * x + m, g_ref[...], b_ref[...])
    state_ref[0] = u[tm - (CONV_W - 1):tm, :]


def _conv_prompt(x, batch, win, cw, wout, g, b):
    n = x.shape[0]
    tm = ROW_TILE
    tiles_per_seq = n // batch // tm
    return pl.pallas_call(
        functools.partial(_conv_prompt_kernel, tm=tm, tiles_per_seq=tiles_per_seq),
        grid=(n // tm,),
        in_specs=[pl.BlockSpec((tm, D_MODEL), lambda i: (i, 0)),
                  _resident(win.shape), _resident(cw.shape), _resident(wout.shape),
                  _resident((1, D_MODEL)), _resident((1, D_MODEL))],
        out_specs=[pl.BlockSpec((tm, D_MODEL), lambda i: (i, 0)),
                   pl.BlockSpec((1, CONV_W - 1, D_MODEL), lambda i: (i // tiles_per_seq, 0, 0))],
        out_shape=[jax.ShapeDtypeStruct((n, D_MODEL), F32),
                   jax.ShapeDtypeStruct((batch, CONV_W - 1, D_MODEL), F32)],
        scratch_shapes=[pltpu.VMEM((tm + CARRY, D_MODEL), F32)],
        compiler_params=_params("arbitrary"),
        name="conv_prompt",
    )(x, win, cw, wout, g, b)


def _conv_sample_kernel(x_ref, p1_ref, p2_ref, win_ref, cw_ref, wout_ref, g_ref, b_ref,
                        y_ref, u_ref, *, seq):
    x = x_ref[...]
    n = x.shape[0]
    xb = x.astype(BF16)
    b_gate = _dot(xb, win_ref[:, 0:D_MODEL])
    c_gate = _dot(xb, win_ref[:, D_MODEL:2 * D_MODEL])
    h = _dot(xb, win_ref[:, 2 * D_MODEL:3 * D_MODEL])
    u = c_gate * h
    pos = lax.broadcasted_iota(jnp.int32, (n, D_MODEL), 0) % seq
    u1 = jnp.where(pos >= 1, pltpu.roll(u, 1, axis=0), p1_ref[...])
    u2 = jnp.where(pos >= 2, pltpu.roll(u, 2, axis=0), p2_ref[...])
    cw = cw_ref[...]
    y = cw[0:1, :] * u2 + cw[1:2, :] * u1 + cw[2:3, :] * u
    m = _dot((b_gate * y).astype(BF16), wout_ref[...])
    y_ref[...] = _layer_norm(ALPHA * x + m, g_ref[...], b_ref[...])
    u_ref[...] = u


def _conv_sample(x, state, seq, win, cw, wout, g, b):
    n = x.shape[0]
    nb = n // seq
    zeros = jnp.zeros((nb, seq, D_MODEL), F32)
    prev1 = zeros.at[:, 0].set(state[:, 1]).reshape(n, D_MODEL)
    prev2 = zeros.at[:, 0].set(state[:, 0]).at[:, 1].set(state[:, 1]).reshape(n, D_MODEL)
    full = _resident((n, D_MODEL))
    y, u = pl.pallas_call(
        functools.partial(_conv_sample_kernel, seq=seq),
        grid=(1,),
        in_specs=[full, full, full, _resident(win.shape), _resident(cw.shape), _resident(wout.shape),
                  _resident((1, D_MODEL)), _resident((1, D_MODEL))],
        out_specs=[full, full],
        out_shape=[jax.ShapeDtypeStruct((n, D_MODEL), F32)] * 2,
        compiler_params=_params("arbitrary"),
        name="conv_sample",
    )(x, prev1, prev2, win, cw, wout, g, b)
    return y, u.reshape(nb, seq, D_MODEL)[:, seq - (CONV_W - 1):]


def _qkv_prompt_kernel(x_ref, w_ref, ktf_ref, vtf_ref, qt_ref, kh_ref, vt_ref, kmean_ref):
    xb = x_ref[...].astype(BF16)
    q = _dot(xb, w_ref[:, 0:D_MODEL])
    k = _dot(xb, w_ref[:, D_MODEL:2 * D_MODEL])
    v = _dot(xb, w_ref[:, 2 * D_MODEL:3 * D_MODEL])
    kmean_ref[0, 0] = jnp.sum(k, axis=0, keepdims=True) * (1.0 / BLOCK)
    for c in range(N_HEADS // HEADS_PER_STEP):
        lanes = slice(c * PAIR_LANES, (c + 1) * PAIR_LANES)
        qt = q[:, lanes].T
        kt = k[:, lanes].T
        vt = v[:, lanes].T
        kh_ref[0, c, 0] = k[:, lanes].astype(BF16)
        for s in range(HEADS_PER_STEP):
            hd = HEADS_PER_STEP * c + s
            rows = slice(s * HEAD_DIM, (s + 1) * HEAD_DIM)
            qt_ref[0, hd] = qt[rows, :]
            ktf_ref[0, hd] = kt[rows, :]
            vtf_ref[0, hd] = vt[rows, :]
            vt_ref[0, hd, 0] = vt[rows, :].astype(BF16)


def _qkv_prompt(x, batch, w):
    n = x.shape[0]
    t = n // batch
    nb = t // BLOCK
    pairs = N_HEADS // HEADS_PER_STEP
    row = pl.BlockSpec((BLOCK, D_MODEL), lambda i: (i, 0))
    feature_major = pl.BlockSpec((1, N_HEADS, HEAD_DIM, BLOCK), lambda i: (i // nb, 0, 0, i % nb))
    return pl.pallas_call(
        _qkv_prompt_kernel,
        grid=(n // BLOCK,),
        in_specs=[row, _resident(w.shape)],
        out_specs=[feature_major, feature_major, feature_major,
                   pl.BlockSpec((1, pairs, 1, BLOCK, PAIR_LANES), lambda i: (i // nb, 0, i % nb, 0, 0)),
                   pl.BlockSpec((1, N_HEADS, 1, HEAD_DIM, BLOCK), lambda i: (i // nb, 0, i % nb, 0, 0)),
                   pl.BlockSpec((1, 1, 1, D_MODEL), lambda i: (i // nb, i % nb, 0, 0))],
        out_shape=[jax.ShapeDtypeStruct((batch, N_HEADS, HEAD_DIM, t), F32),
                   jax.ShapeDtypeStruct((batch, N_HEADS, HEAD_DIM, t), F32),
                   jax.ShapeDtypeStruct((batch, N_HEADS, HEAD_DIM, t), F32),
                   jax.ShapeDtypeStruct((batch, pairs, nb, BLOCK, PAIR_LANES), BF16),
                   jax.ShapeDtypeStruct((batch, N_HEADS, nb, HEAD_DIM, BLOCK), BF16),
                   jax.ShapeDtypeStruct((batch, nb, 1, D_MODEL), F32)],
        compiler_params=_params("parallel"),
        name="qkv_prompt",
    )(x, w)


def _matmul_kernel(x_ref, w_ref, y_ref):
    y_ref[...] = _dot(x_ref[...].astype(BF16), w_ref[...])


def _matmul(x, w):
    n, m = x.shape[0], w.shape[1]
    return pl.pallas_call(
        _matmul_kernel,
        grid=(1,),
        in_specs=[_resident(x.shape), _resident(w.shape)],
        out_specs=_resident((n, m)),
        out_shape=jax.ShapeDtypeStruct((n, m), F32),
        compiler_params=_params("arbitrary"),
        name="matmul",
    )(x, w)


def _select_topk(gate, blk, axis, n_blocks):
    sel = jnp.zeros(gate.shape, F32)
    for _ in range(TOP_K):
        mx = jnp.max(gate, axis=axis, keepdims=True)
        idx = jnp.min(jnp.where(gate == mx, blk, n_blocks), axis=axis, keepdims=True)
        hit = blk == idx
        sel = jnp.where(hit & (mx > -jnp.inf), 1.0, sel)
        gate = jnp.where(hit, -jnp.inf, gate)
    return sel


def _attn_prompt_kernel(slope_ref, qt_ref, k_ref, vt_ref, kmean_ref, ot_ref, sel_ref,
                        s_slots, p_slots):
    i = pl.program_id(2)
    nb = k_ref.shape[2]
    key = lax.broadcasted_iota(jnp.int32, (BLOCK, BLOCK), 0)
    qry = lax.broadcasted_iota(jnp.int32, (BLOCK, BLOCK), 1)
    rel = (qry - key).astype(F32)
    blk = lax.broadcasted_iota(jnp.int32, (nb, BLOCK), 0)
    zeros = jnp.zeros((HEAD_DIM, BLOCK), BF16)

    weights, biases = [], []
    for s in range(HEADS_PER_STEP):
        slope2 = slope_ref[s] * LOG2E
        qt = qt_ref[0, s]
        gate = jnp.dot(kmean_ref[0, s], qt, precision=lax.Precision.HIGHEST,
                       preferred_element_type=F32)
        gate = jnp.where(blk < i, gate, -jnp.inf)
        sel = _select_topk(gate, blk, 0, nb)
        shift = (i - blk).astype(F32) * (-BLOCK * slope2)
        sel_ref[s] = jnp.where(sel > 0.0, shift, NEG)
        qb = (qt * (HEAD_DIM ** -0.5 * LOG2E)).astype(BF16)
        pieces = [qb if r == s else zeros for r in range(HEADS_PER_STEP)]
        weights.append(jnp.concatenate(pieces, axis=0))
        biases.append(rel * (-slope2))

    def produce(j, slot, causal=False):
        kj = k_ref[0, 0, jnp.minimum(j, nb - 1)]
        maxima = []
        for s in range(HEADS_PER_STEP):
            t = _dot(kj, weights[s]) + biases[s]
            if causal:
                t = jnp.where(key <= qry, t, NEG)
            slot[s] = t
            maxima.append(jnp.max(t, axis=0, keepdims=True))
        return maxima

    def values(j, pslot):
        jj = jnp.minimum(j, nb - 1)
        return [_dot(vt_ref[0, s, jj], pslot[s]) for s in range(HEADS_PER_STEP)]

    def rows_of(j):
        return [sel_ref[s, pl.ds(j, 1), :] for s in range(HEADS_PER_STEP)]

    N_ST = 5

    def step(state, maxima, rows, k4, pv):
        out = []
        for s in range(HEADS_PER_STEP):
            m, l, acc, alpha_2, alpha_1 = state[N_ST * s:N_ST * s + N_ST]
            m_new = jnp.maximum(m, maxima[s] + rows[s])
            alpha = jnp.exp2(m - m_new)
            p = jnp.exp2(s_slots[k4, s] + (rows[s] - m_new))
            p_slots[k4, s] = p.astype(BF16)
            l = alpha * l + jnp.sum(p, axis=0, keepdims=True)
            if pv is not None:
                acc = alpha_2 * acc + pv[s]
            out += [m_new, l, acc, alpha_1, alpha]
        return out

    def block_of(k):
        return jnp.where(k <= 0, i, k - 1)

    one = jnp.ones((1, BLOCK), F32)
    state = []
    for s in range(HEADS_PER_STEP):
        state += [jnp.full((1, BLOCK), NEG, F32), jnp.zeros((1, BLOCK), F32),
                  jnp.zeros((HEAD_DIM, BLOCK), F32), one, one]
    p_slots[3] = jnp.zeros(p_slots.shape[1:], BF16)
    max_0 = produce(i, s_slots.at[0], causal=True)
    max_a = produce(0, s_slots.at[1])
    max_b = produce(1, s_slots.at[2])
    no_shift = [jnp.zeros((1, BLOCK), F32)] * HEADS_PER_STEP
    state = step(state, max_0, no_shift, 0, None)
    n_state = len(state)
    hps = HEADS_PER_STEP

    def body(g, carry):
        state = list(carry[:n_state])
        pending = [list(carry[n_state:n_state + hps]), list(carry[n_state + hps:])]
        k = TRIP_STAGES * g + 1
        for u in range(TRIP_STAGES):
            ahead = (u + 3) % 4
            pending.append(produce(k + u + 1, s_slots.at[ahead]))
            pv = values(block_of(k + u - 2), p_slots.at[ahead])
            state = step(state, pending[u], rows_of(k + u - 1), (u + 1) % 4, pv)
        return tuple(state) + tuple(pending[TRIP_STAGES]) + tuple(pending[TRIP_STAGES + 1])

    trips = (i + TRIP_STAGES - 1) // TRIP_STAGES
    carry = lax.fori_loop(0, trips, body, tuple(state) + tuple(max_a) + tuple(max_b))
    last = TRIP_STAGES * trips
    pv_2 = values(block_of(last - 1), p_slots.at[3])
    pv_1 = values(block_of(last), p_slots.at[0])
    for s in range(HEADS_PER_STEP):
        _, l, acc, alpha_2, alpha_1 = carry[N_ST * s:N_ST * s + N_ST]
        ot_ref[0, s] = (alpha_1 * (alpha_2 * acc + pv_2[s]) + pv_1[s]) / l


def _attn_prompt(slopes, qt, kh, vt, kmean):
    batch, _, _, t = qt.shape
    nb = t // BLOCK
    assert nb % TRIP_STAGES == 0
    hps = HEADS_PER_STEP
    slope_lanes = jnp.broadcast_to(slopes[:, None, None], (N_HEADS, 1, BLOCK))
    return pl.pallas_call(
        _attn_prompt_kernel,
        grid=(batch, N_HEADS // hps, nb),
        in_specs=[pl.BlockSpec((hps, 1, BLOCK), lambda b, h, i: (h, 0, 0)),
                  pl.BlockSpec((1, hps, HEAD_DIM, BLOCK), lambda b, h, i: (b, h, 0, i)),
                  pl.BlockSpec((1, 1, nb, BLOCK, PAIR_LANES), lambda b, h, i: (b, h, 0, 0, 0)),
                  pl.BlockSpec((1, hps, nb, HEAD_DIM, BLOCK), lambda b, h, i: (b, h, 0, 0, 0)),
                  pl.BlockSpec((1, hps, nb, HEAD_DIM), lambda b, h, i: (b, h, 0, 0))],
        out_specs=pl.BlockSpec((1, hps, HEAD_DIM, BLOCK), lambda b, h, i: (b, h, 0, i)),
        out_shape=jax.ShapeDtypeStruct((batch, N_HEADS, HEAD_DIM, t), F32),
        scratch_shapes=[pltpu.VMEM((hps, nb, BLOCK), F32),
                        pltpu.VMEM((4, hps, BLOCK, BLOCK), F32),
                        pltpu.VMEM((4, hps, BLOCK, BLOCK), BF16)],
        compiler_params=_params("parallel", "parallel", "arbitrary"),
        name="attn_prompt",
    )(slope_lanes, qt, kh, vt, kmean)


def _page_matrix(page_ref):
    return page_ref[0, 0].reshape(D_MODEL, PAGE_SIZE)


def _page_specs(layer, n_pages):
    per_step = STEP_BLOCKS * PAGES_PER_BLOCK

    def spec(which):
        return pl.BlockSpec((1, 1, N_HEADS, HEAD_DIM, PAGE_SIZE),
                            lambda b, j, pt: (layer, pt[b * n_pages + per_step * j + which], 0, 0, 0))

    return [spec(w) for w in range(per_step)]


def _head_rows(q4, seq):
    rows = jnp.concatenate([jnp.broadcast_to(q4[t:t + 1, :], (N_HEADS, D_MODEL)) for t in range(seq)], axis=0)
    r = lax.broadcasted_iota(jnp.int32, rows.shape, 0)
    lane = lax.broadcasted_iota(jnp.int32, rows.shape, 1)
    return jnp.where(r % N_HEADS == lane // HEAD_DIM, rows, 0.0)


def _sample_scores_kernel(pt_ref, *refs, seq, n_blocks):
    n_pages = STEP_BLOCKS * PAGES_PER_BLOCK
    page_refs = refs[:n_pages]
    q_ref, knew_ref, slope_ref, p_ref, pown_ref, s_all, kmean_t = refs[n_pages:]
    j = pl.program_id(1)
    rows = seq * N_HEADS
    qbd = _head_rows(q_ref[0], seq)
    qb = (qbd * (HEAD_DIM ** -0.5)).astype(BF16)

    @pl.when(j == 0)
    def _():
        kmean_t[...] = jnp.zeros(kmean_t.shape, F32)

    lane = lax.broadcasted_iota(jnp.int32, kmean_t.shape, 1)
    for u in range(STEP_BLOCKS):
        blk = j * STEP_BLOCKS + u
        ksum = jnp.zeros((D_MODEL, 1), F32)
        for w in range(PAGES_PER_BLOCK):
            kp = _page_matrix(page_refs[u * PAGES_PER_BLOCK + w])
            ksum = ksum + jnp.sum(kp, axis=1, keepdims=True)
            s_all[blk, :, w * PAGE_SIZE:(w + 1) * PAGE_SIZE] = _dot(qb, kp.astype(BF16))
        kmean_t[...] = jnp.where(lane == blk, ksum * (1.0 / BLOCK), kmean_t[...])

    @pl.when(j == n_blocks // STEP_BLOCKS - 1)
    def _():
        slope = slope_ref[...]
        gate = jnp.dot(qbd, kmean_t[...], precision=lax.Precision.HIGHEST,
                       preferred_element_type=F32)
        blk = lax.broadcasted_iota(jnp.int32, gate.shape, 1)
        gate = jnp.where(blk < n_blocks, gate, -jnp.inf)
        sel = _select_topk(gate, blk, 1, n_blocks)
        past = n_blocks * BLOCK
        tok = lax.broadcasted_iota(jnp.int32, (rows, 1), 0) // N_HEADS
        lane = lax.broadcasted_iota(jnp.int32, (rows, BLOCK), 1)

        knew = knew_ref[0]
        own_lane = lax.broadcasted_iota(jnp.int32, (rows, PAGE_SIZE), 1)
        s_own = jnp.full((rows, PAGE_SIZE), NEG, F32)
        for s in range(seq):
            dots = jnp.sum(qbd * knew[s:s + 1, :], axis=1, keepdims=True) * (HEAD_DIM ** -0.5)
            logit = dots - slope * (tok - s).astype(F32)
            s_own = jnp.where((own_lane == s) & (tok >= s), logit, s_own)
        m = jnp.max(s_own, axis=1, keepdims=True)

        def logits(b):
            dist = (past + tok - b * BLOCK - lane).astype(F32)
            return jnp.where(sel[:, b:b + 1] > 0.0, s_all[b] - slope * dist, NEG)

        for b in range(n_blocks):
            m = jnp.maximum(m, jnp.max(logits(b), axis=1, keepdims=True))
        p_own = jnp.exp(s_own - m)
        l = jnp.sum(p_own, axis=1, keepdims=True)
        for b in range(n_blocks):
            l = l + jnp.sum(jnp.exp(logits(b) - m), axis=1, keepdims=True)
        inv = 1.0 / l
        pown_ref[0] = p_own * inv
        for b in range(n_blocks):
            p_ref[0, b] = (jnp.exp(logits(b) - m) * inv).astype(BF16)


def _sample_scores(page_table, cache, layer, q, knew, slope_rows):
    nbatch, seq, _ = q.shape
    n_pages = page_table.shape[1]
    n_blocks = n_pages // PAGES_PER_BLOCK
    rows = seq * N_HEADS
    grid_spec = pltpu.PrefetchScalarGridSpec(
        num_scalar_prefetch=1,
        grid=(nbatch, n_blocks // STEP_BLOCKS),
        in_specs=[*_page_specs(layer, n_pages),
                  pl.BlockSpec((1, seq, D_MODEL), lambda b, j, pt: (b, 0, 0)),
                  pl.BlockSpec((1, seq, D_MODEL), lambda b, j, pt: (b, 0, 0)),
                  pl.BlockSpec((rows, 1), lambda b, j, pt: (0, 0))],
        out_specs=[pl.BlockSpec((1, n_blocks, rows, BLOCK), lambda b, j, pt: (b, 0, 0, 0)),
                   pl.BlockSpec((1, rows, PAGE_SIZE), lambda b, j, pt: (b, 0, 0))],
        scratch_shapes=[pltpu.VMEM((n_blocks, rows, BLOCK), F32),
                        pltpu.VMEM((D_MODEL, max(n_blocks, PAGE_SIZE)), F32)],
    )
    return pl.pallas_call(
        functools.partial(_sample_scores_kernel, seq=seq, n_blocks=n_blocks),
        grid_spec=grid_spec,
        out_shape=[jax.ShapeDtypeStruct((nbatch, n_blocks, rows, BLOCK), BF16),
                   jax.ShapeDtypeStruct((nbatch, rows, PAGE_SIZE), F32)],
        compiler_params=_params("parallel", "arbitrary"),
        name="sample_scores",
    )(page_table.reshape(-1), *[cache] * (STEP_BLOCKS * PAGES_PER_BLOCK), q, knew, slope_rows)


def _sample_values_kernel(pt_ref, *refs, seq, n_blocks):
    n_pages = STEP_BLOCKS * PAGES_PER_BLOCK
    page_refs = refs[:n_pages]
    p_ref, pown_ref, vnew_ref, o_ref, acc = refs[n_pages:]
    j = pl.program_id(1)

    @pl.when(j == 0)
    def _():
        acc[...] = jnp.zeros(acc.shape, F32)

    total = jnp.zeros(acc.shape, F32)
    for u in range(STEP_BLOCKS):
        p = p_ref[0, u]
        for w in range(PAGES_PER_BLOCK):
            vp = _page_matrix(page_refs[u * PAGES_PER_BLOCK + w]).astype(BF16)
            total = total + _dot_nt(p[:, w * PAGE_SIZE:(w + 1) * PAGE_SIZE], vp)
    acc[...] += total

    @pl.when(j == n_blocks // STEP_BLOCKS - 1)
    def _():
        o = acc[...]
        p_own = pown_ref[0]
        vnew = vnew_ref[0]
        for s in range(seq):
            o = o + p_own[:, s:s + 1] * vnew[s:s + 1, :]
        r = lax.broadcasted_iota(jnp.int32, o.shape, 0)
        lane = lax.broadcasted_iota(jnp.int32, o.shape, 1)
        o = jnp.where(r % N_HEADS == lane // HEAD_DIM, o, 0.0)
        o_ref[0] = jnp.sum(o.reshape(seq, N_HEADS, D_MODEL), axis=1)


def _sample_values(page_table, cache, layer, p, p_own, vnew):
    nbatch, n_blocks, rows, _ = p.shape
    seq = rows // N_HEADS
    n_pages = page_table.shape[1]
    grid_spec = pltpu.PrefetchScalarGridSpec(
        num_scalar_prefetch=1,
        grid=(nbatch, n_blocks // STEP_BLOCKS),
        in_specs=[*_page_specs(layer, n_pages),
                  pl.BlockSpec((1, STEP_BLOCKS, rows, BLOCK), lambda b, j, pt: (b, j, 0, 0)),
                  pl.BlockSpec((1, rows, PAGE_SIZE), lambda b, j, pt: (b, 0, 0)),
                  pl.BlockSpec((1, seq, D_MODEL), lambda b, j, pt: (b, 0, 0))],
        out_specs=pl.BlockSpec((1, seq, D_MODEL), lambda b, j, pt: (b, 0, 0)),
        scratch_shapes=[pltpu.VMEM((rows, D_MODEL), F32)],
    )
    return pl.pallas_call(
        functools.partial(_sample_values_kernel, seq=seq, n_blocks=n_blocks),
        grid_spec=grid_spec,
        out_shape=jax.ShapeDtypeStruct((nbatch, seq, D_MODEL), F32),
        compiler_params=_params("parallel", "arbitrary"),
        name="sample_values",
    )(page_table.reshape(-1), *[cache] * (STEP_BLOCKS * PAGES_PER_BLOCK), p, p_own, vnew)


def kernel(x_prompt, x_sample, cache_k, cache_v, state_conv, page_table, w_in_conv, conv_w, w_out_conv,
           w_qkv, w_o, w_gate_up, w_down, ln_g, ln_b):
    batch, seq_p, _ = x_prompt.shape
    nbatch, seq_s, _ = x_sample.shape
    assert seq_p % ROW_TILE == 0 and seq_p % BLOCK == 0 and seq_s >= CONV_W - 1
    assert page_table.shape[1] % (STEP_BLOCKS * PAGES_PER_BLOCK) == 0

    slopes = 2.0 ** (-8.0 * jnp.arange(1, N_HEADS + 1, dtype=F32) / N_HEADS)
    slope_rows = jnp.tile(slopes, seq_s).reshape(seq_s * N_HEADS, 1)

    cache_k = cache_k.transpose(0, 1, 3, 4, 2)
    cache_v = cache_v.transpose(0, 1, 3, 4, 2)

    xp = x_prompt.reshape(batch * seq_p, D_MODEL)
    xs = x_sample.reshape(nbatch * seq_s, D_MODEL)
    k_p, v_p, k_s, v_s, conv_p, conv_s = [], [], [], [], [], []
    for i in range(DEPTH):
        li = i // 2
        g0, b0 = ln_g[i, 0:1], ln_b[i, 0:1]
        g1, b1 = ln_g[i, 1:2], ln_b[i, 1:2]
        if i % 2 == 0:
            win, wout = w_in_conv[li].astype(BF16), w_out_conv[li].astype(BF16)
            xp, cp = _conv_prompt(xp, batch, win, conv_w[li], wout, g0, b0)
            xs, cs = _conv_sample(xs, state_conv[li], seq_s, win, conv_w[li], wout, g0, b0)
            conv_p.append(cp)
            conv_s.append(cs)
        else:
            wqkv, wo = w_qkv[li].astype(BF16), w_o[li].astype(BF16)
            ktf, vtf, qt, kh, vt, kmean = _qkv_prompt(xp, batch, wqkv)
            kmean = kmean.reshape(batch, -1, N_HEADS, HEAD_DIM).transpose(0, 2, 1, 3)
            ot = _attn_prompt(slopes, qt, kh, vt, kmean)
            xp = _proj_ln(xp, ot.reshape(batch, D_MODEL, seq_p), wo, g0, b0)
            k_p.append(ktf)
            v_p.append(vtf)

            qkv = _matmul(xs, wqkv).reshape(nbatch, seq_s, 3, D_MODEL)
            q, kn, vn = qkv[:, :, 0], qkv[:, :, 1], qkv[:, :, 2]
            p, p_own = _sample_scores(page_table, cache_k, li, q, kn, slope_rows)
            o = _sample_values(page_table, cache_v, li, p, p_own, vn)
            xs = _proj_ln(xs, o.reshape(nbatch * seq_s, D_MODEL), wo, g0, b0)
            k_s.append(kn.reshape(nbatch, seq_s, N_HEADS, HEAD_DIM))
            v_s.append(vn.reshape(nbatch, seq_s, N_HEADS, HEAD_DIM))
        wgu, wd = w_gate_up[i].astype(BF16), w_down[i].astype(BF16)
        xp = _ffn_ln(xp, wgu, wd, g1, b1)
        xs = _ffn_ln(xs, wgu, wd, g1, b1)
    return (xp.reshape(batch, seq_p, D_MODEL), xs.reshape(nbatch, seq_s, D_MODEL),
            jnp.stack(k_p).transpose(0, 1, 4, 2, 3), jnp.stack(v_p).transpose(0, 1, 4, 2, 3), jnp.stack(conv_p),
            jnp.stack(k_s), jnp.stack(v_s), jnp.stack(conv_s))
```

```python
import functools

import jax
import jax.numpy as jnp
from jax import lax
from jax.experimental import pallas as pl
from jax.experimental.pallas import tpu as pltpu

D_MODEL = 1024
DEPTH = 4
N_HEADS = 16
HEAD_DIM = D_MODEL // N_HEADS
CONV_W = 3
BLOCK = 256
TOP_K = 3
PAGE_SIZE = 128
PAGES_PER_BLOCK = BLOCK // PAGE_SIZE
STEP_BLOCKS = 8
D_FF = -(-8 * D_MODEL // 768) * 256
ALPHA = (2 * DEPTH) ** 0.25
LN_EPS = 1e-5
NEG = -1e30
LOG2E = 1.4426950408889634
HEADS_PER_STEP = 2
PAIR_LANES = HEADS_PER_STEP * HEAD_DIM
TRIP_STAGES = 4

VMEM_LIMIT_BYTES = 56 * 1024 * 1024
ROW_TILE = 512
FF_CHUNKS = ((0, 768), (768, 1536), (1536, 2304), (2304, D_FF))

BF16 = jnp.bfloat16
F32 = jnp.float32


def _params(*sem):
    return pltpu.CompilerParams(dimension_semantics=sem, vmem_limit_bytes=VMEM_LIMIT_BYTES)


def _dot(a, b):
    return jnp.dot(a, b, preferred_element_type=F32)


def _dot_nt(a, b, precision=None):
    return lax.dot_general(a, b, (((1,), (1,)), ((), ())), precision=precision,
                           preferred_element_type=F32)


def _layer_norm(z, g, b):
    mu = jnp.mean(z, axis=-1, keepdims=True)
    zc = z - mu
    var = jnp.mean(zc * zc, axis=-1, keepdims=True)
    return zc * lax.rsqrt(var + LN_EPS) * g + b


def _resident(shape):
    zeros = (0,) * len(shape)
    return pl.BlockSpec(shape, lambda *_: zeros)


def _ffn_ln_kernel(x_ref, wgu_ref, wd_ref, g_ref, b_ref, y_ref):
    x = x_ref[...]
    xb = x.astype(BF16)
    acc = ALPHA * x
    for c0, c1 in FF_CHUNKS:
        gate = _dot(xb, wgu_ref[:, c0:c1])
        up = _dot(xb, wgu_ref[:, D_FF + c0:D_FF + c1])
        h = (gate * jax.nn.sigmoid(gate) * up).astype(BF16)
        acc = acc + _dot(h, wd_ref[c0:c1, :])
    y_ref[...] = _layer_norm(acc, g_ref[...], b_ref[...])


def _ffn_ln(x, wgu, wd, g, b):
    n = x.shape[0]
    tm = min(ROW_TILE, n)
    return pl.pallas_call(
        _ffn_ln_kernel,
        grid=(n // tm,),
        in_specs=[pl.BlockSpec((tm, D_MODEL), lambda i: (i, 0)),
                  _resident(wgu.shape), _resident(wd.shape),
                  _resident((1, D_MODEL)), _resident((1, D_MODEL))],
        out_specs=pl.BlockSpec((tm, D_MODEL), lambda i: (i, 0)),
        out_shape=jax.ShapeDtypeStruct((n, D_MODEL), F32),
        compiler_params=_params("parallel"),
        name="ffn_ln",
    )(x, wgu, wd, g, b)


def _proj_ln_kernel(x_ref, a_ref, w_ref, g_ref, b_ref, y_ref, *, a_transposed):
    a = a_ref[0].T if a_transposed else a_ref[...]
    z = ALPHA * x_ref[...] + _dot(a.astype(BF16), w_ref[...])
    y_ref[...] = _layer_norm(z, g_ref[...], b_ref[...])


def _proj_ln(x, a, w, g, b):
    n = x.shape[0]
    tm = min(ROW_TILE, n)
    if a.ndim == 3:
        tiles_per_seq = a.shape[2] // tm
        a_spec = pl.BlockSpec((1, D_MODEL, tm), lambda i: (i // tiles_per_seq, 0, i % tiles_per_seq))
    else:
        a_spec = pl.BlockSpec((tm, D_MODEL), lambda i: (i, 0))
    return pl.pallas_call(
        functools.partial(_proj_ln_kernel, a_transposed=a.ndim == 3),
        grid=(n // tm,),
        in_specs=[pl.BlockSpec((tm, D_MODEL), lambda i: (i, 0)), a_spec,
                  _resident(w.shape), _resident((1, D_MODEL)), _resident((1, D_MODEL))],
        out_specs=pl.BlockSpec((tm, D_MODEL), lambda i: (i, 0)),
        out_shape=jax.ShapeDtypeStruct((n, D_MODEL), F32),
        compiler_params=_params("parallel"),
        name="proj_ln",
    )(x, a, w, g, b)


CARRY = 8


def _conv_prompt_kernel(x_ref, win_ref, cw_ref, wout_ref, g_ref, b_ref, y_ref, state_ref, ubuf,
                        *, tm, tiles_per_seq):
    i = pl.program_id(0)
    x = x_ref[...]
    xb = x.astype(BF16)
    b_gate = _dot(xb, win_ref[:, 0:D_MODEL])
    c_gate = _dot(xb, win_ref[:, D_MODEL:2 * D_MODEL])
    h = _dot(xb, win_ref[:, 2 * D_MODEL:3 * D_MODEL])
    u = c_gate * h

    @pl.when(i % tiles_per_seq == 0)
    def _():
        ubuf[0:CARRY, :] = jnp.zeros((CARRY, D_MODEL), F32)

    @pl.when(i % tiles_per_seq != 0)
    def _():
        ubuf[0:CARRY, :] = ubuf[tm:tm + CARRY, :]

    ubuf[CARRY:CARRY + tm, :] = u
    cw = cw_ref[...]
    y = cw[2:3, :] * u
    for j in range(CONV_W - 1):
        off = CARRY - (CONV_W - 1) + j
        y = y + cw[j:j + 1, :] * ubuf[off:off + tm, :]
    m = _dot((b_gate * y).astype(BF16), wout_ref[...])
    y_ref[...] = _layer_norm(ALPHA * x + m, g_ref[...], b_ref[...])
    state_ref[0] = u[tm - (CONV_W - 1):tm, :]


def _conv_prompt(x, batch, win, cw, wout, g, b):
    n = x.shape[0]
    tm = ROW_TILE
    tiles_per_seq = n // batch // tm
    return pl.pallas_call(
        functools.partial(_conv_prompt_kernel, tm=tm, tiles_per_seq=tiles_per_seq),
        grid=(n // tm,),
        in_specs=[pl.BlockSpec((tm, D_MODEL), lambda i: (i, 0)),
                  _resident(win.shape), _resident(cw.shape), _resident(wout.shape),
                  _resident((1, D_MODEL)), _resident((1, D_MODEL))],
        out_specs=[pl.BlockSpec((tm, D_MODEL), lambda i: (i, 0)),
                   pl.BlockSpec((1, CONV_W - 1, D_MODEL), lambda i: (i // tiles_per_seq, 0, 0))],
        out_shape=[jax.ShapeDtypeStruct((n, D_MODEL), F32),
                   jax.ShapeDtypeStruct((batch, CONV_W - 1, D_MODEL), F32)],
        scratch_shapes=[pltpu.VMEM((tm + CARRY, D_MODEL), F32)],
        compiler_params=_params("arbitrary"),
        name="conv_prompt",
    )(x, win, cw, wout, g, b)


def _conv_sample_kernel(x_ref, p1_ref, p2_ref, win_ref, cw_ref, wout_ref, g_ref, b_ref,
                        y_ref, u_ref, *, seq):
    x = x_ref[...]
    n = x.shape[0]
    xb = x.astype(BF16)
    b_gate = _dot(xb, win_ref[:, 0:D_MODEL])
    c_gate = _dot(xb, win_ref[:, D_MODEL:2 * D_MODEL])
    h = _dot(xb, win_ref[:, 2 * D_MODEL:3 * D_MODEL])
    u = c_gate * h
    pos = lax.broadcasted_iota(jnp.int32, (n, D_MODEL), 0) % seq
    u1 = jnp.where(pos >= 1, pltpu.roll(u, 1, axis=0), p1_ref[...])
    u2 = jnp.where(pos >= 2, pltpu.roll(u, 2, axis=0), p2_ref[...])
    cw = cw_ref[...]
    y = cw[0:1, :] * u2 + cw[1:2, :] * u1 + cw[2:3, :] * u
    m = _dot((b_gate * y).astype(BF16), wout_ref[...])
    y_ref[...] = _layer_norm(ALPHA * x + m, g_ref[...], b_ref[...])
    u_ref[...] = u


def _conv_sample(x, state, seq, win, cw, wout, g, b):
    n = x.shape[0]
    nb = n // seq
    zeros = jnp.zeros((nb, seq, D_MODEL), F32)
    prev1 = zeros.at[:, 0].set(state[:, 1]).reshape(n, D_MODEL)
    prev2 = zeros.at[:, 0].set(state[:, 0]).at[:, 1].set(state[:, 1]).reshape(n, D_MODEL)
    full = _resident((n, D_MODEL))
    y, u = pl.pallas_call(
        functools.partial(_conv_sample_kernel, seq=seq),
        grid=(1,),
        in_specs=[full, full, full, _resident(win.shape), _resident(cw.shape), _resident(wout.shape),
                  _resident((1, D_MODEL)), _resident((1, D_MODEL))],
        out_specs=[full, full],
        out_shape=[jax.ShapeDtypeStruct((n, D_MODEL), F32)] * 2,
        compiler_params=_params("arbitrary"),
        name="conv_sample",
    )(x, prev1, prev2, win, cw, wout, g, b)
    return y, u.reshape(nb, seq, D_MODEL)[:, seq - (CONV_W - 1):]


def _qkv_prompt_kernel(x_ref, w_ref, ktf_ref, vtf_ref, qt_ref, kh_ref, vt_ref, kmean_ref):
    xb = x_ref[...].astype(BF16)
    q = _dot(xb, w_ref[:, 0:D_MODEL])
    k = _dot(xb, w_ref[:, D_MODEL:2 * D_MODEL])
    v = _dot(xb, w_ref[:, 2 * D_MODEL:3 * D_MODEL])
    kmean_ref[0, 0] = jnp.sum(k, axis=0, keepdims=True) * (1.0 / BLOCK)
    for c in range(N_HEADS // HEADS_PER_STEP):
        lanes = slice(c * PAIR_LANES, (c + 1) * PAIR_LANES)
        qt = q[:, lanes].T
        kt = k[:, lanes].T
        vt = v[:, lanes].T
        kh_ref[0, c, 0] = k[:, lanes].astype(BF16)
        for s in range(HEADS_PER_STEP):
            hd = HEADS_PER_STEP * c + s
            rows = slice(s * HEAD_DIM, (s + 1) * HEAD_DIM)
            qt_ref[0, hd] = qt[rows, :]
            ktf_ref[0, hd] = kt[rows, :]
            vtf_ref[0, hd] = vt[rows, :]
            vt_ref[0, hd, 0] = vt[rows, :].astype(BF16)


def _qkv_prompt(x, batch, w):
    n = x.shape[0]
    t = n // batch
    nb = t // BLOCK
    pairs = N_HEADS // HEADS_PER_STEP
    row = pl.BlockSpec((BLOCK, D_MODEL), lambda i: (i, 0))
    feature_major = pl.BlockSpec((1, N_HEADS, HEAD_DIM, BLOCK), lambda i: (i // nb, 0, 0, i % nb))
    return pl.pallas_call(
        _qkv_prompt_kernel,
        grid=(n // BLOCK,),
        in_specs=[row, _resident(w.shape)],
        out_specs=[feature_major, feature_major, feature_major,
                   pl.BlockSpec((1, pairs, 1, BLOCK, PAIR_LANES), lambda i: (i // nb, 0, i % nb, 0, 0)),
                   pl.BlockSpec((1, N_HEADS, 1, HEAD_DIM, BLOCK), lambda i: (i // nb, 0, i % nb, 0, 0)),
                   pl.BlockSpec((1, 1, 1, D_MODEL), lambda i: (i // nb, i % nb, 0, 0))],
        out_shape=[jax.ShapeDtypeStruct((batch, N_HEADS, HEAD_DIM, t), F32),
                   jax.ShapeDtypeStruct((batch, N_HEADS, HEAD_DIM, t), F32),
                   jax.ShapeDtypeStruct((batch, N_HEADS, HEAD_DIM, t), F32),
                   jax.ShapeDtypeStruct((batch, pairs, nb, BLOCK, PAIR_LANES), BF16),
                   jax.ShapeDtypeStruct((batch, N_HEADS, nb, HEAD_DIM, BLOCK), BF16),
                   jax.ShapeDtypeStruct((batch, nb, 1, D_MODEL), F32)],
        compiler_params=_params("parallel"),
        name="qkv_prompt",
    )(x, w)


def _matmul_kernel(x_ref, w_ref, y_ref):
    y_ref[...] = _dot(x_ref[...].astype(BF16), w_ref[...])


def _matmul(x, w):
    n, m = x.shape[0], w.shape[1]
    return pl.pallas_call(
        _matmul_kernel,
        grid=(1,),
        in_specs=[_resident(x.shape), _resident(w.shape)],
        out_specs=_resident((n, m)),
        out_shape=jax.ShapeDtypeStruct((n, m), F32),
        compiler_params=_params("arbitrary"),
        name="matmul",
    )(x, w)


def _select_topk(gate, blk, axis, n_blocks):
    sel = jnp.zeros(gate.shape, F32)
    for _ in range(TOP_K):
        mx = jnp.max(gate, axis=axis, keepdims=True)
        idx = jnp.min(jnp.where(gate == mx, blk, n_blocks), axis=axis, keepdims=True)
        hit = blk == idx
        sel = jnp.where(hit & (mx > -jnp.inf), 1.0, sel)
        gate = jnp.where(hit, -jnp.inf, gate)
    return sel


def _attn_prompt_kernel(slope_ref, qt_ref, k_ref, vt_ref, kmean_ref, ot_ref, sel_ref,
                        s_slots, p_slots):
    i = pl.program_id(2)
    nb = k_ref.shape[2]
    key = lax.broadcasted_iota(jnp.int32, (BLOCK, BLOCK), 0)
    qry = lax.broadcasted_iota(jnp.int32, (BLOCK, BLOCK), 1)
    rel = (qry - key).astype(F32)
    blk = lax.broadcasted_iota(jnp.int32, (nb, BLOCK), 0)
    zeros = jnp.zeros((HEAD_DIM, BLOCK), BF16)

    weights, biases = [], []
    for s in range(HEADS_PER_STEP):
        slope2 = slope_ref[s] * LOG2E
        qt = qt_ref[0, s]
        gate = jnp.dot(kmean_ref[0, s], qt, precision=lax.Precision.HIGHEST,
                       preferred_element_type=F32)
        gate = jnp.where(blk < i, gate, -jnp.inf)
        sel = _select_topk(gate, blk, 0, nb)
        shift = (i - blk).astype(F32) * (-BLOCK * slope2)
        sel_ref[s] = jnp.where(sel > 0.0, shift, NEG)
        qb = (qt * (HEAD_DIM ** -0.5 * LOG2E)).astype(BF16)
        pieces = [qb if r == s else zeros for r in range(HEADS_PER_STEP)]
        weights.append(jnp.concatenate(pieces, axis=0))
        biases.append(rel * (-slope2))

    def produce(j, slot, causal=False):
        kj = k_ref[0, 0, jnp.minimum(j, nb - 1)]
        maxima = []
        for s in range(HEADS_PER_STEP):
            t = _dot(kj, weights[s]) + biases[s]
            if causal:
                t = jnp.where(key <= qry, t, NEG)
            slot[s] = t
            maxima.append(jnp.max(t, axis=0, keepdims=True))
        return maxima

    def values(j, pslot):
        jj = jnp.minimum(j, nb - 1)
        return [_dot(vt_ref[0, s, jj], pslot[s]) for s in range(HEADS_PER_STEP)]

    def rows_of(j):
        return [sel_ref[s, pl.ds(j, 1), :] for s in range(HEADS_PER_STEP)]

    N_ST = 5

    def step(state, maxima, rows, k4, pv):
        out = []
        for s in range(HEADS_PER_STEP):
            m, l, acc, alpha_2, alpha_1 = state[N_ST * s:N_ST * s + N_ST]
            m_new = jnp.maximum(m, maxima[s] + rows[s])
            alpha = jnp.exp2(m - m_new)
            p = jnp.exp2(s_slots[k4, s] + (rows[s] - m_new))
            p_slots[k4, s] = p.astype(BF16)
            l = alpha * l + jnp.sum(p, axis=0, keepdims=True)
            if pv is not None:
                acc = alpha_2 * acc + pv[s]
            out += [m_new, l, acc, alpha_1, alpha]
        return out

    def block_of(k):
        return jnp.where(k <= 0, i, k - 1)

    one = jnp.ones((1, BLOCK), F32)
    state = []
    for s in range(HEADS_PER_STEP):
        state += [jnp.full((1, BLOCK), NEG, F32), jnp.zeros((1, BLOCK), F32),
                  jnp.zeros((HEAD_DIM, BLOCK), F32), one, one]
    p_slots[3] = jnp.zeros(p_slots.shape[1:], BF16)
    max_0 = produce(i, s_slots.at[0], causal=True)
    max_a = produce(0, s_slots.at[1])
    max_b = produce(1, s_slots.at[2])
    no_shift = [jnp.zeros((1, BLOCK), F32)] * HEADS_PER_STEP
    state = step(state, max_0, no_shift, 0, None)
    n_state = len(state)
    hps = HEADS_PER_STEP

    def body(g, carry):
        state = list(carry[:n_state])
        pending = [list(carry[n_state:n_state + hps]), list(carry[n_state + hps:])]
        k = TRIP_STAGES * g + 1
        for u in range(TRIP_STAGES):
            ahead = (u + 3) % 4
            pending.append(produce(k + u + 1, s_slots.at[ahead]))
            pv = values(block_of(k + u - 2), p_slots.at[ahead])
            state = step(state, pending[u], rows_of(k + u - 1), (u + 1) % 4, pv)
        return tuple(state) + tuple(pending[TRIP_STAGES]) + tuple(pending[TRIP_STAGES + 1])

    trips = (i + TRIP_STAGES - 1) // TRIP_STAGES
    carry = lax.fori_loop(0, trips, body, tuple(state) + tuple(max_a) + tuple(max_b))
    last = TRIP_STAGES * trips
    pv_2 = values(block_of(last - 1), p_slots.at[3])
    pv_1 = values(block_of(last), p_slots.at[0])
    for s in range(HEADS_PER_STEP):
        _, l, acc, alpha_2, alpha_1 = carry[N_ST * s:N_ST * s + N_ST]
        ot_ref[0, s] = (alpha_1 * (alpha_2 * acc + pv_2[s]) + pv_1[s]) / l


def _attn_prompt(slopes, qt, kh, vt, kmean):
    batch, _, _, t = qt.shape
    nb = t // BLOCK
    assert nb % TRIP_STAGES == 0
    hps = HEADS_PER_STEP
    slope_lanes = jnp.broadcast_to(slopes[:, None, None], (N_HEADS, 1, BLOCK))
    return pl.pallas_call(
        _attn_prompt_kernel,
        grid=(batch, N_HEADS // hps, nb),
        in_specs=[pl.BlockSpec((hps, 1, BLOCK), lambda b, h, i: (h, 0, 0)),
                  pl.BlockSpec((1, hps, HEAD_DIM, BLOCK), lambda b, h, i: (b, h, 0, i)),
                  pl.BlockSpec((1, 1, nb, BLOCK, PAIR_LANES), lambda b, h, i: (b, h, 0, 0, 0)),
                  pl.BlockSpec((1, hps, nb, HEAD_DIM, BLOCK), lambda b, h, i: (b, h, 0, 0, 0)),
                  pl.BlockSpec((1, hps, nb, HEAD_DIM), lambda b, h, i: (b, h, 0, 0))],
        out_specs=pl.BlockSpec((1, hps, HEAD_DIM, BLOCK), lambda b, h, i: (b, h, 0, i)),
        out_shape=jax.ShapeDtypeStruct((batch, N_HEADS, HEAD_DIM, t), F32),
        scratch_shapes=[pltpu.VMEM((hps, nb, BLOCK), F32),
                        pltpu.VMEM((4, hps, BLOCK, BLOCK), F32),
                        pltpu.VMEM((4, hps, BLOCK, BLOCK), BF16)],
        compiler_params=_params("parallel", "parallel", "arbitrary"),
        name="attn_prompt",
    )(slope_lanes, qt, kh, vt, kmean)


def _page_matrix(page_ref):
    return page_ref[0, 0].reshape(D_MODEL, PAGE_SIZE)


def _page_specs(layer, n_pages):
    per_step = STEP_BLOCKS * PAGES_PER_BLOCK

    def spec(which):
        return pl.BlockSpec((1, 1, N_HEADS, HEAD_DIM, PAGE_SIZE),
                            lambda b, j, pt: (layer, pt[b * n_pages + per_step * j + which], 0, 0, 0))

    return [spec(w) for w in range(per_step)]


def _head_rows(q4, seq):
    rows = jnp.concatenate([jnp.broadcast_to(q4[t:t + 1, :], (N_HEADS, D_MODEL)) for t in range(seq)], axis=0)
    r = lax.broadcasted_iota(jnp.int32, rows.shape, 0)
    lane = lax.broadcasted_iota(jnp.int32, rows.shape, 1)
    return jnp.where(r % N_HEADS == lane // HEAD_DIM, rows, 0.0)


def _sample_scores_kernel(pt_ref, *refs, seq, n_blocks):
    n_pages = STEP_BLOCKS * PAGES_PER_BLOCK
    page_refs = refs[:n_pages]
    q_ref, knew_ref, slope_ref, p_ref, pown_ref, s_all, kmean_t = refs[n_pages:]
    j = pl.program_id(1)
    rows = seq * N_HEADS
    qbd = _head_rows(q_ref[0], seq)
    qb = (qbd * (HEAD_DIM ** -0.5)).astype(BF16)

    @pl.when(j == 0)
    def _():
        kmean_t[...] = jnp.zeros(kmean_t.shape, F32)

    lane = lax.broadcasted_iota(jnp.int32, kmean_t.shape, 1)
    for u in range(STEP_BLOCKS):
        blk = j * STEP_BLOCKS + u
        ktot = jnp.zeros((D_MODEL, PAGE_SIZE), F32)
        for w in range(PAGES_PER_BLOCK):
            kp = _page_matrix(page_refs[u * PAGES_PER_BLOCK + w])
            ktot = ktot + kp
            s_all[blk, :, w * PAGE_SIZE:(w + 1) * PAGE_SIZE] = _dot(qb, kp.astype(BF16))
        kmean = jnp.sum(ktot, axis=1, keepdims=True) * (1.0 / BLOCK)
        kmean_t[...] = jnp.where(lane == blk, kmean, kmean_t[...])

    @pl.when(j == n_blocks // STEP_BLOCKS - 1)
    def _():
        slope = slope_ref[...]
        gate = jnp.dot(qbd, kmean_t[...], precision=lax.Precision.HIGHEST,
                       preferred_element_type=F32)
        blk = lax.broadcasted_iota(jnp.int32, gate.shape, 1)
        gate = jnp.where(blk < n_blocks, gate, -jnp.inf)
        sel = _select_topk(gate, blk, 1, n_blocks)
        past = n_blocks * BLOCK
        tok = lax.broadcasted_iota(jnp.int32, (rows, 1), 0) // N_HEADS
        lane = lax.broadcasted_iota(jnp.int32, (rows, BLOCK), 1)

        knew = knew_ref[0]
        own_lane = lax.broadcasted_iota(jnp.int32, (rows, PAGE_SIZE), 1)
        s_own = jnp.full((rows, PAGE_SIZE), NEG, F32)
        for s in range(seq):
            dots = jnp.sum(qbd * knew[s:s + 1, :], axis=1, keepdims=True) * (HEAD_DIM ** -0.5)
            logit = dots - slope * (tok - s).astype(F32)
            s_own = jnp.where((own_lane == s) & (tok >= s), logit, s_own)
        m = jnp.max(s_own, axis=1, keepdims=True)

        def logits(b):
            dist = (past + tok - b * BLOCK - lane).astype(F32)
            return jnp.where(sel[:, b:b + 1] > 0.0, s_all[b] - slope * dist, NEG)

        for b in range(n_blocks):
            m = jnp.maximum(m, jnp.max(logits(b), axis=1, keepdims=True))
        p_own = jnp.exp(s_own - m)
        l = jnp.sum(p_own, axis=1, keepdims=True)
        for b in range(n_blocks):
            l = l + jnp.sum(jnp.exp(logits(b) - m), axis=1, keepdims=True)
        inv = 1.0 / l
        pown_ref[0] = p_own * inv
        for b in range(n_blocks):
            p_ref[0, b] = (jnp.exp(logits(b) - m) * inv).astype(BF16)


def _sample_scores(page_table, cache, layer, q, knew, slope_rows):
    nbatch, seq, _ = q.shape
    n_pages = page_table.shape[1]
    n_blocks = n_pages // PAGES_PER_BLOCK
    rows = seq * N_HEADS
    grid_spec = pltpu.PrefetchScalarGridSpec(
        num_scalar_prefetch=1,
        grid=(nbatch, n_blocks // STEP_BLOCKS),
        in_specs=[*_page_specs(layer, n_pages),
                  pl.BlockSpec((1, seq, D_MODEL), lambda b, j, pt: (b, 0, 0)),
                  pl.BlockSpec((1, seq, D_MODEL), lambda b, j, pt: (b, 0, 0)),
                  pl.BlockSpec((rows, 1), lambda b, j, pt: (0, 0))],
        out_specs=[pl.BlockSpec((1, n_blocks, rows, BLOCK), lambda b, j, pt: (b, 0, 0, 0)),
                   pl.BlockSpec((1, rows, PAGE_SIZE), lambda b, j, pt: (b, 0, 0))],
        scratch_shapes=[pltpu.VMEM((n_blocks, rows, BLOCK), F32),
                        pltpu.VMEM((D_MODEL, max(n_blocks, PAGE_SIZE)), F32)],
    )
    return pl.pallas_call(
        functools.partial(_sample_scores_kernel, seq=seq, n_blocks=n_blocks),
        grid_spec=grid_spec,
        out_shape=[jax.ShapeDtypeStruct((nbatch, n_blocks, rows, BLOCK), BF16),
                   jax.ShapeDtypeStruct((nbatch, rows, PAGE_SIZE), F32)],
        compiler_params=_params("parallel", "arbitrary"),
        name="sample_scores",
    )(page_table.reshape(-1), *[cache] * (STEP_BLOCKS * PAGES_PER_BLOCK), q, knew, slope_rows)


def _sample_values_kernel(pt_ref, *refs, seq, n_blocks):
    n_pages = STEP_BLOCKS * PAGES_PER_BLOCK
    page_refs = refs[:n_pages]
    p_ref, pown_ref, vnew_ref, o_ref, acc = refs[n_pages:]
    j = pl.program_id(1)

    @pl.when(j == 0)
    def _():
        acc[...] = jnp.zeros(acc.shape, F32)

    total = jnp.zeros(acc.shape, F32)
    for u in range(STEP_BLOCKS):
        p = p_ref[0, u]
        for w in range(PAGES_PER_BLOCK):
            vp = _page_matrix(page_refs[u * PAGES_PER_BLOCK + w]).astype(BF16)
            total = total + _dot_nt(p[:, w * PAGE_SIZE:(w + 1) * PAGE_SIZE], vp)
    acc[...] += total

    @pl.when(j == n_blocks // STEP_BLOCKS - 1)
    def _():
        o = acc[...]
        p_own = pown_ref[0]
        vnew = vnew_ref[0]
        for s in range(seq):
            o = o + p_own[:, s:s + 1] * vnew[s:s + 1, :]
        r = lax.broadcasted_iota(jnp.int32, o.shape, 0)
        lane = lax.broadcasted_iota(jnp.int32, o.shape, 1)
        o = jnp.where(r % N_HEADS == lane // HEAD_DIM, o, 0.0)
        o_ref[0] = jnp.sum(o.reshape(seq, N_HEADS, D_MODEL), axis=1)


def _sample_values(page_table, cache, layer, p, p_own, vnew):
    nbatch, n_blocks, rows, _ = p.shape
    seq = rows // N_HEADS
    n_pages = page_table.shape[1]
    grid_spec = pltpu.PrefetchScalarGridSpec(
        num_scalar_prefetch=1,
        grid=(nbatch, n_blocks // STEP_BLOCKS),
        in_specs=[*_page_specs(layer, n_pages),
                  pl.BlockSpec((1, STEP_BLOCKS, rows, BLOCK), lambda b, j, pt: (b, j, 0, 0)),
                  pl.BlockSpec((1, rows, PAGE_SIZE), lambda b, j, pt: (b, 0, 0)),
                  pl.BlockSpec((1, seq, D_MODEL), lambda b, j, pt: (b, 0, 0))],
        out_specs=pl.BlockSpec((1, seq, D_MODEL), lambda b, j, pt: (b, 0, 0)),
        scratch_shapes=[pltpu.VMEM((rows, D_MODEL), F32)],
    )
    return pl.pallas_call(
        functools.partial(_sample_values_kernel, seq=seq, n_blocks=n_blocks),
        grid_spec=grid_spec,
        out_shape=jax.ShapeDtypeStruct((nbatch, seq, D_MODEL), F32),
        compiler_params=_params("parallel", "arbitrary"),
        name="sample_values",
    )(page_table.reshape(-1), *[cache] * (STEP_BLOCKS * PAGES_PER_BLOCK), p, p_own, vnew)


def kernel(x_prompt, x_sample, cache_k, cache_v, state_conv, page_table, w_in_conv, conv_w, w_out_conv,
           w_qkv, w_o, w_gate_up, w_down, ln_g, ln_b):
    batch, seq_p, _ = x_prompt.shape
    nbatch, seq_s, _ = x_sample.shape
    assert seq_p % ROW_TILE == 0 and seq_p % BLOCK == 0 and seq_s >= CONV_W - 1
    assert page_table.shape[1] % (STEP_BLOCKS * PAGES_PER_BLOCK) == 0

    slopes = 2.0 ** (-8.0 * jnp.arange(1, N_HEADS + 1, dtype=F32) / N_HEADS)
    slope_rows = jnp.tile(slopes, seq_s).reshape(seq_s * N_HEADS, 1)

    cache_k = cache_k.transpose(0, 1, 3, 4, 2)
    cache_v = cache_v.transpose(0, 1, 3, 4, 2)

    xp = x_prompt.reshape(batch * seq_p, D_MODEL)
    xs = x_sample.reshape(nbatch * seq_s, D_MODEL)
    k_p, v_p, k_s, v_s, conv_p, conv_s = [], [], [], [], [], []
    for i in range(DEPTH):
        li = i // 2
        g0, b0 = ln_g[i, 0:1], ln_b[i, 0:1]
        g1, b1 = ln_g[i, 1:2], ln_b[i, 1:2]
        if i % 2 == 0:
            win, wout = w_in_conv[li].astype(BF16), w_out_conv[li].astype(BF16)
            xp, cp = _conv_prompt(xp, batch, win, conv_w[li], wout, g0, b0)
            xs, cs = _conv_sample(xs, state_conv[li], seq_s, win, conv_w[li], wout, g0, b0)
            conv_p.append(cp)
            conv_s.append(cs)
        else:
            wqkv, wo = w_qkv[li].astype(BF16), w_o[li].astype(BF16)
            ktf, vtf, qt, kh, vt, kmean = _qkv_prompt(xp, batch, wqkv)
            kmean = kmean.reshape(batch, -1, N_HEADS, HEAD_DIM).transpose(0, 2, 1, 3)
            ot = _attn_prompt(slopes, qt, kh, vt, kmean)
            xp = _proj_ln(xp, ot.reshape(batch, D_MODEL, seq_p), wo, g0, b0)
            k_p.append(ktf)
            v_p.append(vtf)

            qkv = _matmul(xs, wqkv).reshape(nbatch, seq_s, 3, D_MODEL)
            q, kn, vn = qkv[:, :, 0], qkv[:, :, 1], qkv[:, :, 2]
            p, p_own = _sample_scores(page_table, cache_k, li, q, kn, slope_rows)
            o = _sample_values(page_table, cache_v, li, p, p_own, vn)
            xs = _proj_ln(xs, o.reshape(nbatch * seq_s, D_MODEL), wo, g0, b0)
            k_s.append(kn.reshape(nbatch, seq_s, N_HEADS, HEAD_DIM))
            v_s.append(vn.reshape(nbatch, seq_s, N_HEADS, HEAD_DIM))
        wgu, wd = w_gate_up[i].astype(BF16), w_down[i].astype(BF16)
        xp = _ffn_ln(xp, wgu, wd, g1, b1)
        xs = _ffn_ln(xs, wgu, wd, g1, b1)
    return (xp.reshape(batch, seq_p, D_MODEL), xs.reshape(nbatch, seq_s, D_MODEL),
            jnp.stack(k_p).transpose(0, 1, 4, 2, 3), jnp.stack(v_p).transpose(0, 1, 4, 2, 3), jnp.stack(conv_p),
            jnp.stack(k_s), jnp.stack(v_s), jnp.stack(conv_s))
```

```python
import functools

import jax
import jax.numpy as jnp
from jax import lax
from jax.experimental import pallas as pl
from jax.experimental.pallas import tpu as pltpu

D_MODEL = 1024
DEPTH = 4
N_HEADS = 16
HEAD_DIM = D_MODEL // N_HEADS
CONV_W = 3
BLOCK = 256
TOP_K = 3
PAGE_SIZE = 128
PAGES_PER_BLOCK = BLOCK // PAGE_SIZE
STEP_BLOCKS = 8
D_FF = -(-8 * D_MODEL // 768) * 256
ALPHA = (2 * DEPTH) ** 0.25
LN_EPS = 1e-5
NEG = -1e30
LOG2E = 1.4426950408889634
HEADS_PER_STEP = 2
PAIR_LANES = HEADS_PER_STEP * HEAD_DIM
V_ROWS = HEAD_DIM + 16
TRIP_STAGES = 4

VMEM_LIMIT_BYTES = 56 * 1024 * 1024
ROW_TILE = 512
FF_CHUNKS = ((0, 768), (768, 1536), (1536, 2304), (2304, D_FF))

BF16 = jnp.bfloat16
F32 = jnp.float32


def _params(*sem):
    return pltpu.CompilerParams(dimension_semantics=sem, vmem_limit_bytes=VMEM_LIMIT_BYTES)


def _dot(a, b):
    return jnp.dot(a, b, preferred_element_type=F32)


def _dot_nt(a, b, precision=None):
    return lax.dot_general(a, b, (((1,), (1,)), ((), ())), precision=precision,
                           preferred_element_type=F32)


def _layer_norm(z, g, b):
    mu = jnp.mean(z, axis=-1, keepdims=True)
    zc = z - mu
    var = jnp.mean(zc * zc, axis=-1, keepdims=True)
    return zc * lax.rsqrt(var + LN_EPS) * g + b


def _resident(shape):
    zeros = (0,) * len(shape)
    return pl.BlockSpec(shape, lambda *_: zeros)


def _ffn_ln_kernel(x_ref, wgu_ref, wd_ref, g_ref, b_ref, y_ref):
    x = x_ref[...]
    xb = x.astype(BF16)
    acc = ALPHA * x
    for c0, c1 in FF_CHUNKS:
        gate = _dot(xb, wgu_ref[:, c0:c1])
        up = _dot(xb, wgu_ref[:, D_FF + c0:D_FF + c1])
        h = (gate * jax.nn.sigmoid(gate) * up).astype(BF16)
        acc = acc + _dot(h, wd_ref[c0:c1, :])
    y_ref[...] = _layer_norm(acc, g_ref[...], b_ref[...])


def _ffn_ln(x, wgu, wd, g, b):
    n = x.shape[0]
    tm = min(ROW_TILE, n)
    return pl.pallas_call(
        _ffn_ln_kernel,
        grid=(n // tm,),
        in_specs=[pl.BlockSpec((tm, D_MODEL), lambda i: (i, 0)),
                  _resident(wgu.shape), _resident(wd.shape),
                  _resident((1, D_MODEL)), _resident((1, D_MODEL))],
        out_specs=pl.BlockSpec((tm, D_MODEL), lambda i: (i, 0)),
        out_shape=jax.ShapeDtypeStruct((n, D_MODEL), F32),
        compiler_params=_params("parallel"),
        name="ffn_ln",
    )(x, wgu, wd, g, b)


def _proj_ln_kernel(x_ref, a_ref, w_ref, g_ref, b_ref, y_ref, *, a_transposed):
    a = a_ref[0].T if a_transposed else a_ref[...]
    z = ALPHA * x_ref[...] + _dot(a.astype(BF16), w_ref[...])
    y_ref[...] = _layer_norm(z, g_ref[...], b_ref[...])


def _proj_ln(x, a, w, g, b):
    n = x.shape[0]
    tm = min(ROW_TILE, n)
    if a.ndim == 3:
        tiles_per_seq = a.shape[2] // tm
        a_spec = pl.BlockSpec((1, D_MODEL, tm), lambda i: (i // tiles_per_seq, 0, i % tiles_per_seq))
    else:
        a_spec = pl.BlockSpec((tm, D_MODEL), lambda i: (i, 0))
    return pl.pallas_call(
        functools.partial(_proj_ln_kernel, a_transposed=a.ndim == 3),
        grid=(n // tm,),
        in_specs=[pl.BlockSpec((tm, D_MODEL), lambda i: (i, 0)), a_spec,
                  _resident(w.shape), _resident((1, D_MODEL)), _resident((1, D_MODEL))],
        out_specs=pl.BlockSpec((tm, D_MODEL), lambda i: (i, 0)),
        out_shape=jax.ShapeDtypeStruct((n, D_MODEL), F32),
        compiler_params=_params("parallel"),
        name="proj_ln",
    )(x, a, w, g, b)


CARRY = 8


def _conv_prompt_kernel(x_ref, win_ref, cw_ref, wout_ref, g_ref, b_ref, y_ref, state_ref, ubuf,
                        *, tm, tiles_per_seq):
    i = pl.program_id(0)
    x = x_ref[...]
    xb = x.astype(BF16)
    b_gate = _dot(xb, win_ref[:, 0:D_MODEL])
    c_gate = _dot(xb, win_ref[:, D_MODEL:2 * D_MODEL])
    h = _dot(xb, win_ref[:, 2 * D_MODEL:3 * D_MODEL])
    u = c_gate * h

    @pl.when(i % tiles_per_seq == 0)
    def _():
        ubuf[0:CARRY, :] = jnp.zeros((CARRY, D_MODEL), F32)

    @pl.when(i % tiles_per_seq != 0)
    def _():
        ubuf[0:CARRY, :] = ubuf[tm:tm + CARRY, :]

    ubuf[CARRY:CARRY + tm, :] = u
    cw = cw_ref[...]
    y = cw[2:3, :] * u
    for j in range(CONV_W - 1):
        off = CARRY - (CONV_W - 1) + j
        y = y + cw[j:j + 1, :] * ubuf[off:off + tm, :]
    m = _dot((b_gate * y).astype(BF16), wout_ref[...])
    y_ref[...] = _layer_norm(ALPHA * x + m, g_ref[...], b_ref[...])
    state_ref[0] = u[tm - (CONV_W - 1):tm, :]


def _conv_prompt(x, batch, win, cw, wout, g, b):
    n = x.shape[0]
    tm = ROW_TILE
    tiles_per_seq = n // batch // tm
    return pl.pallas_call(
        functools.partial(_conv_prompt_kernel, tm=tm, tiles_per_seq=tiles_per_seq),
        grid=(n // tm,),
        in_specs=[pl.BlockSpec((tm, D_MODEL), lambda i: (i, 0)),
                  _resident(win.shape), _resident(cw.shape), _resident(wout.shape),
                  _resident((1, D_MODEL)), _resident((1, D_MODEL))],
        out_specs=[pl.BlockSpec((tm, D_MODEL), lambda i: (i, 0)),
                   pl.BlockSpec((1, CONV_W - 1, D_MODEL), lambda i: (i // tiles_per_seq, 0, 0))],
        out_shape=[jax.ShapeDtypeStruct((n, D_MODEL), F32),
                   jax.ShapeDtypeStruct((batch, CONV_W - 1, D_MODEL), F32)],
        scratch_shapes=[pltpu.VMEM((tm + CARRY, D_MODEL), F32)],
        compiler_params=_params("arbitrary"),
        name="conv_prompt",
    )(x, win, cw, wout, g, b)


def _conv_sample_kernel(x_ref, p1_ref, p2_ref, win_ref, cw_ref, wout_ref, g_ref, b_ref,
                        y_ref, u_ref, *, seq):
    x = x_ref[...]
    n = x.shape[0]
    xb = x.astype(BF16)
    b_gate = _dot(xb, win_ref[:, 0:D_MODEL])
    c_gate = _dot(xb, win_ref[:, D_MODEL:2 * D_MODEL])
    h = _dot(xb, win_ref[:, 2 * D_MODEL:3 * D_MODEL])
    u = c_gate * h
    pos = lax.broadcasted_iota(jnp.int32, (n, D_MODEL), 0) % seq
    u1 = jnp.where(pos >= 1, pltpu.roll(u, 1, axis=0), p1_ref[...])
    u2 = jnp.where(pos >= 2, pltpu.roll(u, 2, axis=0), p2_ref[...])
    cw = cw_ref[...]
    y = cw[0:1, :] * u2 + cw[1:2, :] * u1 + cw[2:3, :] * u
    m = _dot((b_gate * y).astype(BF16), wout_ref[...])
    y_ref[...] = _layer_norm(ALPHA * x + m, g_ref[...], b_ref[...])
    u_ref[...] = u


def _conv_sample(x, state, seq, win, cw, wout, g, b):
    n = x.shape[0]
    nb = n // seq
    zeros = jnp.zeros((nb, seq, D_MODEL), F32)
    prev1 = zeros.at[:, 0].set(state[:, 1]).reshape(n, D_MODEL)
    prev2 = zeros.at[:, 0].set(state[:, 0]).at[:, 1].set(state[:, 1]).reshape(n, D_MODEL)
    full = _resident((n, D_MODEL))
    y, u = pl.pallas_call(
        functools.partial(_conv_sample_kernel, seq=seq),
        grid=(1,),
        in_specs=[full, full, full, _resident(win.shape), _resident(cw.shape), _resident(wout.shape),
                  _resident((1, D_MODEL)), _resident((1, D_MODEL))],
        out_specs=[full, full],
        out_shape=[jax.ShapeDtypeStruct((n, D_MODEL), F32)] * 2,
        compiler_params=_params("arbitrary"),
        name="conv_sample",
    )(x, prev1, prev2, win, cw, wout, g, b)
    return y, u.reshape(nb, seq, D_MODEL)[:, seq - (CONV_W - 1):]


def _qkv_prompt_kernel(x_ref, w_ref, ktf_ref, vtf_ref, qt_ref, kh_ref, vt_ref, kmean_ref):
    xb = x_ref[...].astype(BF16)
    q = _dot(xb, w_ref[:, 0:D_MODEL])
    k = _dot(xb, w_ref[:, D_MODEL:2 * D_MODEL])
    v = _dot(xb, w_ref[:, 2 * D_MODEL:3 * D_MODEL])
    kmean_ref[0, 0] = jnp.sum(k, axis=0, keepdims=True) * (1.0 / BLOCK)
    for c in range(N_HEADS // HEADS_PER_STEP):
        lanes = slice(c * PAIR_LANES, (c + 1) * PAIR_LANES)
        qt = q[:, lanes].T
        kt = k[:, lanes].T
        vt = v[:, lanes].T
        kh_ref[0, c, 0] = k[:, lanes].astype(BF16)
        for s in range(HEADS_PER_STEP):
            hd = HEADS_PER_STEP * c + s
            rows = slice(s * HEAD_DIM, (s + 1) * HEAD_DIM)
            qt_ref[0, hd] = qt[rows, :]
            ktf_ref[0, hd] = kt[rows, :]
            vtf_ref[0, hd] = vt[rows, :]
            vt_ref[0, hd, 0, 0:HEAD_DIM] = vt[rows, :].astype(BF16)
            vt_ref[0, hd, 0, HEAD_DIM:V_ROWS] = jnp.ones((V_ROWS - HEAD_DIM, BLOCK), BF16)


def _qkv_prompt(x, batch, w):
    n = x.shape[0]
    t = n // batch
    nb = t // BLOCK
    pairs = N_HEADS // HEADS_PER_STEP
    row = pl.BlockSpec((BLOCK, D_MODEL), lambda i: (i, 0))
    feature_major = pl.BlockSpec((1, N_HEADS, HEAD_DIM, BLOCK), lambda i: (i // nb, 0, 0, i % nb))
    return pl.pallas_call(
        _qkv_prompt_kernel,
        grid=(n // BLOCK,),
        in_specs=[row, _resident(w.shape)],
        out_specs=[feature_major, feature_major, feature_major,
                   pl.BlockSpec((1, pairs, 1, BLOCK, PAIR_LANES), lambda i: (i // nb, 0, i % nb, 0, 0)),
                   pl.BlockSpec((1, N_HEADS, 1, V_ROWS, BLOCK), lambda i: (i // nb, 0, i % nb, 0, 0)),
                   pl.BlockSpec((1, 1, 1, D_MODEL), lambda i: (i // nb, i % nb, 0, 0))],
        out_shape=[jax.ShapeDtypeStruct((batch, N_HEADS, HEAD_DIM, t), F32),
                   jax.ShapeDtypeStruct((batch, N_HEADS, HEAD_DIM, t), F32),
                   jax.ShapeDtypeStruct((batch, N_HEADS, HEAD_DIM, t), F32),
                   jax.ShapeDtypeStruct((batch, pairs, nb, BLOCK, PAIR_LANES), BF16),
                   jax.ShapeDtypeStruct((batch, N_HEADS, nb, V_ROWS, BLOCK), BF16),
                   jax.ShapeDtypeStruct((batch, nb, 1, D_MODEL), F32)],
        compiler_params=_params("parallel"),
        name="qkv_prompt",
    )(x, w)


def _matmul_kernel(x_ref, w_ref, y_ref):
    y_ref[...] = _dot(x_ref[...].astype(BF16), w_ref[...])


def _matmul(x, w):
    n, m = x.shape[0], w.shape[1]
    return pl.pallas_call(
        _matmul_kernel,
        grid=(1,),
        in_specs=[_resident(x.shape), _resident(w.shape)],
        out_specs=_resident((n, m)),
        out_shape=jax.ShapeDtypeStruct((n, m), F32),
        compiler_params=_params("arbitrary"),
        name="matmul",
    )(x, w)


def _select_topk(gate, blk, axis, n_blocks):
    sel = jnp.zeros(gate.shape, F32)
    for _ in range(TOP_K):
        mx = jnp.max(gate, axis=axis, keepdims=True)
        idx = jnp.min(jnp.where(gate == mx, blk, n_blocks), axis=axis, keepdims=True)
        hit = blk == idx
        sel = jnp.where(hit & (mx > -jnp.inf), 1.0, sel)
        gate = jnp.where(hit, -jnp.inf, gate)
    return sel


def _attn_prompt_kernel(slope_ref, qt_ref, k_ref, vt_ref, kmean_ref, ot_ref, sel_ref,
                        s_slots, p_slots):
    i = pl.program_id(2)
    nb = k_ref.shape[2]
    key = lax.broadcasted_iota(jnp.int32, (BLOCK, BLOCK), 0)
    qry = lax.broadcasted_iota(jnp.int32, (BLOCK, BLOCK), 1)
    rel = (qry - key).astype(F32)
    blk = lax.broadcasted_iota(jnp.int32, (nb, BLOCK), 0)
    zeros = jnp.zeros((HEAD_DIM, BLOCK), BF16)

    weights, biases = [], []
    for s in range(HEADS_PER_STEP):
        slope2 = slope_ref[s] * LOG2E
        qt = qt_ref[0, s]
        gate = jnp.dot(kmean_ref[0, s], qt, precision=lax.Precision.HIGHEST,
                       preferred_element_type=F32)
        gate = jnp.where(blk < i, gate, -jnp.inf)
        sel = _select_topk(gate, blk, 0, nb)
        shift = (i - blk).astype(F32) * (-BLOCK * slope2)
        sel_ref[s] = jnp.where(sel > 0.0, shift, NEG)
        qb = (qt * (HEAD_DIM ** -0.5 * LOG2E)).astype(BF16)
        pieces = [qb if r == s else zeros for r in range(HEADS_PER_STEP)]
        weights.append(jnp.concatenate(pieces, axis=0))
        biases.append(rel * (-slope2))

    def produce(j, slot, causal=False):
        kj = k_ref[0, 0, jnp.minimum(j, nb - 1)]
        maxima = []
        for s in range(HEADS_PER_STEP):
            t = _dot(kj, weights[s]) + biases[s]
            if causal:
                t = jnp.where(key <= qry, t, NEG)
            slot[s] = t
            maxima.append(jnp.max(t, axis=0, keepdims=True))
        return maxima

    def values(j, pslot):
        jj = jnp.minimum(j, nb - 1)
        return [_dot(vt_ref[0, s, jj], pslot[s]) for s in range(HEADS_PER_STEP)]

    def rows_of(j):
        return [sel_ref[s, pl.ds(j, 1), :] for s in range(HEADS_PER_STEP)]

    N_ST = 4

    def step(state, maxima, rows, k4, pv):
        out = []
        for s in range(HEADS_PER_STEP):
            m, acc, alpha_2, alpha_1 = state[N_ST * s:N_ST * s + N_ST]
            m_new = jnp.maximum(m, maxima[s] + rows[s])
            alpha = jnp.exp2(m - m_new)
            p_slots[k4, s] = jnp.exp2(s_slots[k4, s] + (rows[s] - m_new)).astype(BF16)
            if pv is not None:
                acc = alpha_2 * acc + pv[s]
            out += [m_new, acc, alpha_1, alpha]
        return out

    def block_of(k):
        return jnp.where(k <= 0, i, k - 1)

    one = jnp.ones((1, BLOCK), F32)
    state = []
    for s in range(HEADS_PER_STEP):
        state += [jnp.full((1, BLOCK), NEG, F32), jnp.zeros((V_ROWS, BLOCK), F32), one, one]
    p_slots[3] = jnp.zeros(p_slots.shape[1:], BF16)
    max_0 = produce(i, s_slots.at[0], causal=True)
    max_a = produce(0, s_slots.at[1])
    max_b = produce(1, s_slots.at[2])
    no_shift = [jnp.zeros((1, BLOCK), F32)] * HEADS_PER_STEP
    state = step(state, max_0, no_shift, 0, None)
    n_state = len(state)
    hps = HEADS_PER_STEP

    def body(g, carry):
        state = list(carry[:n_state])
        pending = [list(carry[n_state:n_state + hps]), list(carry[n_state + hps:])]
        k = TRIP_STAGES * g + 1
        for u in range(TRIP_STAGES):
            ahead = (u + 3) % 4
            pending.append(produce(k + u + 1, s_slots.at[ahead]))
            pv = values(block_of(k + u - 2), p_slots.at[ahead])
            state = step(state, pending[u], rows_of(k + u - 1), (u + 1) % 4, pv)
        return tuple(state) + tuple(pending[TRIP_STAGES]) + tuple(pending[TRIP_STAGES + 1])

    trips = (i + TRIP_STAGES - 1) // TRIP_STAGES
    carry = lax.fori_loop(0, trips, body, tuple(state) + tuple(max_a) + tuple(max_b))
    last = TRIP_STAGES * trips
    pv_2 = values(block_of(last - 1), p_slots.at[3])
    pv_1 = values(block_of(last), p_slots.at[0])
    for s in range(HEADS_PER_STEP):
        _, acc, alpha_2, alpha_1 = carry[N_ST * s:N_ST * s + N_ST]
        total = alpha_1 * (alpha_2 * acc + pv_2[s]) + pv_1[s]
        ot_ref[0, s] = total[0:HEAD_DIM] / total[HEAD_DIM:HEAD_DIM + 1]


def _attn_prompt(slopes, qt, kh, vt, kmean):
    batch, _, _, t = qt.shape
    nb = t // BLOCK
    assert nb % TRIP_STAGES == 0
    hps = HEADS_PER_STEP
    slope_lanes = jnp.broadcast_to(slopes[:, None, None], (N_HEADS, 1, BLOCK))
    return pl.pallas_call(
        _attn_prompt_kernel,
        grid=(batch, N_HEADS // hps, nb),
        in_specs=[pl.BlockSpec((hps, 1, BLOCK), lambda b, h, i: (h, 0, 0)),
                  pl.BlockSpec((1, hps, HEAD_DIM, BLOCK), lambda b, h, i: (b, h, 0, i)),
                  pl.BlockSpec((1, 1, nb, BLOCK, PAIR_LANES), lambda b, h, i: (b, h, 0, 0, 0)),
                  pl.BlockSpec((1, hps, nb, V_ROWS, BLOCK), lambda b, h, i: (b, h, 0, 0, 0)),
                  pl.BlockSpec((1, hps, nb, HEAD_DIM), lambda b, h, i: (b, h, 0, 0))],
        out_specs=pl.BlockSpec((1, hps, HEAD_DIM, BLOCK), lambda b, h, i: (b, h, 0, i)),
        out_shape=jax.ShapeDtypeStruct((batch, N_HEADS, HEAD_DIM, t), F32),
        scratch_shapes=[pltpu.VMEM((hps, nb, BLOCK), F32),
                        pltpu.VMEM((4, hps, BLOCK, BLOCK), F32),
                        pltpu.VMEM((4, hps, BLOCK, BLOCK), BF16)],
        compiler_params=_params("parallel", "parallel", "arbitrary"),
        name="attn_prompt",
    )(slope_lanes, qt, kh, vt, kmean)


def _page_matrix(page_ref):
    return page_ref[0, 0].reshape(D_MODEL, PAGE_SIZE)


def _page_specs(layer, n_pages):
    per_step = STEP_BLOCKS * PAGES_PER_BLOCK

    def spec(which):
        return pl.BlockSpec((1, 1, N_HEADS, HEAD_DIM, PAGE_SIZE),
                            lambda b, j, pt: (layer, pt[b * n_pages + per_step * j + which], 0, 0, 0))

    return [spec(w) for w in range(per_step)]


def _head_rows(q4, seq):
    rows = jnp.concatenate([jnp.broadcast_to(q4[t:t + 1, :], (N_HEADS, D_MODEL)) for t in range(seq)], axis=0)
    r = lax.broadcasted_iota(jnp.int32, rows.shape, 0)
    lane = lax.broadcasted_iota(jnp.int32, rows.shape, 1)
    return jnp.where(r % N_HEADS == lane // HEAD_DIM, rows, 0.0)


def _sample_scores_kernel(pt_ref, *refs, seq, n_blocks):
    n_pages = STEP_BLOCKS * PAGES_PER_BLOCK
    page_refs = refs[:n_pages]
    q_ref, knew_ref, slope_ref, p_ref, pown_ref, s_all, kmean_t = refs[n_pages:]
    j = pl.program_id(1)
    rows = seq * N_HEADS
    qbd = _head_rows(q_ref[0], seq)
    qb = (qbd * (HEAD_DIM ** -0.5)).astype(BF16)

    @pl.when(j == 0)
    def _():
        kmean_t[...] = jnp.zeros(kmean_t.shape, F32)

    lane = lax.broadcasted_iota(jnp.int32, kmean_t.shape, 1)
    for u in range(STEP_BLOCKS):
        blk = j * STEP_BLOCKS + u
        ktot = jnp.zeros((D_MODEL, PAGE_SIZE), F32)
        for w in range(PAGES_PER_BLOCK):
            kp = _page_matrix(page_refs[u * PAGES_PER_BLOCK + w])
            ktot = ktot + kp
            s_all[blk, :, w * PAGE_SIZE:(w + 1) * PAGE_SIZE] = _dot(qb, kp.astype(BF16))
        kmean = jnp.sum(ktot, axis=1, keepdims=True) * (1.0 / BLOCK)
        kmean_t[...] = jnp.where(lane == blk, kmean, kmean_t[...])

    @pl.when(j == n_blocks // STEP_BLOCKS - 1)
    def _():
        slope = slope_ref[...]
        gate = jnp.dot(qbd, kmean_t[...], precision=lax.Precision.HIGHEST,
                       preferred_element_type=F32)
        blk = lax.broadcasted_iota(jnp.int32, gate.shape, 1)
        gate = jnp.where(blk < n_blocks, gate, -jnp.inf)
        sel = _select_topk(gate, blk, 1, n_blocks)
        past = n_blocks * BLOCK
        tok = lax.broadcasted_iota(jnp.int32, (rows, 1), 0) // N_HEADS
        lane = lax.broadcasted_iota(jnp.int32, (rows, BLOCK), 1)

        knew = knew_ref[0]
        own_lane = lax.broadcasted_iota(jnp.int32, (rows, PAGE_SIZE), 1)
        s_own = jnp.full((rows, PAGE_SIZE), NEG, F32)
        for s in range(seq):
            dots = jnp.sum(qbd * knew[s:s + 1, :], axis=1, keepdims=True) * (HEAD_DIM ** -0.5)
            logit = dots - slope * (tok - s).astype(F32)
            s_own = jnp.where((own_lane == s) & (tok >= s), logit, s_own)
        m = jnp.max(s_own, axis=1, keepdims=True)

        def logits(b):
            dist = (past + tok - b * BLOCK - lane).astype(F32)
            return jnp.where(sel[:, b:b + 1] > 0.0, s_all[b] - slope * dist, NEG)

        for b in range(n_blocks):
            m = jnp.maximum(m, jnp.max(logits(b), axis=1, keepdims=True))
        p_own = jnp.exp(s_own - m)
        l = jnp.sum(p_own, axis=1, keepdims=True)
        for b in range(n_blocks):
            l = l + jnp.sum(jnp.exp(logits(b) - m), axis=1, keepdims=True)
        inv = 1.0 / l
        pown_ref[0] = p_own * inv
        for b in range(n_blocks):
            p_ref[0, b] = (jnp.exp(logits(b) - m) * inv).astype(BF16)


def _sample_scores(page_table, cache, layer, q, knew, slope_rows):
    nbatch, seq, _ = q.shape
    n_pages = page_table.shape[1]
    n_blocks = n_pages // PAGES_PER_BLOCK
    rows = seq * N_HEADS
    grid_spec = pltpu.PrefetchScalarGridSpec(
        num_scalar_prefetch=1,
        grid=(nbatch, n_blocks // STEP_BLOCKS),
        in_specs=[*_page_specs(layer, n_pages),
                  pl.BlockSpec((1, seq, D_MODEL), lambda b, j, pt: (b, 0, 0)),
                  pl.BlockSpec((1, seq, D_MODEL), lambda b, j, pt: (b, 0, 0)),
                  pl.BlockSpec((rows, 1), lambda b, j, pt: (0, 0))],
        out_specs=[pl.BlockSpec((1, n_blocks, rows, BLOCK), lambda b, j, pt: (b, 0, 0, 0)),
                   pl.BlockSpec((1, rows, PAGE_SIZE), lambda b, j, pt: (b, 0, 0))],
        scratch_shapes=[pltpu.VMEM((n_blocks, rows, BLOCK), F32),
                        pltpu.VMEM((D_MODEL, max(n_blocks, PAGE_SIZE)), F32)],
    )
    return pl.pallas_call(
        functools.partial(_sample_scores_kernel, seq=seq, n_blocks=n_blocks),
        grid_spec=grid_spec,
        out_shape=[jax.ShapeDtypeStruct((nbatch, n_blocks, rows, BLOCK), BF16),
                   jax.ShapeDtypeStruct((nbatch, rows, PAGE_SIZE), F32)],
        compiler_params=_params("parallel", "arbitrary"),
        name="sample_scores",
    )(page_table.reshape(-1), *[cache] * (STEP_BLOCKS * PAGES_PER_BLOCK), q, knew, slope_rows)


def _sample_values_kernel(pt_ref, *refs, seq, n_blocks):
    n_pages = STEP_BLOCKS * PAGES_PER_BLOCK
    page_refs = refs[:n_pages]
    p_ref, pown_ref, vnew_ref, o_ref, acc = refs[n_pages:]
    j = pl.program_id(1)

    @pl.when(j == 0)
    def _():
        acc[...] = jnp.zeros(acc.shape, F32)

    total = jnp.zeros(acc.shape, F32)
    for u in range(STEP_BLOCKS):
        p = p_ref[0, u]
        for w in range(PAGES_PER_BLOCK):
            vp = _page_matrix(page_refs[u * PAGES_PER_BLOCK + w]).astype(BF16)
            total = total + _dot_nt(p[:, w * PAGE_SIZE:(w + 1) * PAGE_SIZE], vp)
    acc[...] += total

    @pl.when(j == n_blocks // STEP_BLOCKS - 1)
    def _():
        o = acc[...]
        p_own = pown_ref[0]
        vnew = vnew_ref[0]
        for s in range(seq):
            o = o + p_own[:, s:s + 1] * vnew[s:s + 1, :]
        r = lax.broadcasted_iota(jnp.int32, o.shape, 0)
        lane = lax.broadcasted_iota(jnp.int32, o.shape, 1)
        o = jnp.where(r % N_HEADS == lane // HEAD_DIM, o, 0.0)
        o_ref[0] = jnp.sum(o.reshape(seq, N_HEADS, D_MODEL), axis=1)


def _sample_values(page_table, cache, layer, p, p_own, vnew):
    nbatch, n_blocks, rows, _ = p.shape
    seq = rows // N_HEADS
    n_pages = page_table.shape[1]
    grid_spec = pltpu.PrefetchScalarGridSpec(
        num_scalar_prefetch=1,
        grid=(nbatch, n_blocks // STEP_BLOCKS),
        in_specs=[*_page_specs(layer, n_pages),
                  pl.BlockSpec((1, STEP_BLOCKS, rows, BLOCK), lambda b, j, pt: (b, j, 0, 0)),
                  pl.BlockSpec((1, rows, PAGE_SIZE), lambda b, j, pt: (b, 0, 0)),
                  pl.BlockSpec((1, seq, D_MODEL), lambda b, j, pt: (b, 0, 0))],
        out_specs=pl.BlockSpec((1, seq, D_MODEL), lambda b, j, pt: (b, 0, 0)),
        scratch_shapes=[pltpu.VMEM((rows, D_MODEL), F32)],
    )
    return pl.pallas_call(
        functools.partial(_sample_values_kernel, seq=seq, n_blocks=n_blocks),
        grid_spec=grid_spec,
        out_shape=jax.ShapeDtypeStruct((nbatch, seq, D_MODEL), F32),
        compiler_params=_params("parallel", "arbitrary"),
        name="sample_values",
    )(page_table.reshape(-1), *[cache] * (STEP_BLOCKS * PAGES_PER_BLOCK), p, p_own, vnew)


def kernel(x_prompt, x_sample, cache_k, cache_v, state_conv, page_table, w_in_conv, conv_w, w_out_conv,
           w_qkv, w_o, w_gate_up, w_down, ln_g, ln_b):
    batch, seq_p, _ = x_prompt.shape
    nbatch, seq_s, _ = x_sample.shape
    assert seq_p % ROW_TILE == 0 and seq_p % BLOCK == 0 and seq_s >= CONV_W - 1
    assert page_table.shape[1] % (STEP_BLOCKS * PAGES_PER_BLOCK) == 0

    slopes = 2.0 ** (-8.0 * jnp.arange(1, N_HEADS + 1, dtype=F32) / N_HEADS)
    slope_rows = jnp.tile(slopes, seq_s).reshape(seq_s * N_HEADS, 1)

    cache_k = cache_k.transpose(0, 1, 3, 4, 2)
    cache_v = cache_v.transpose(0, 1, 3, 4, 2)

    xp = x_prompt.reshape(batch * seq_p, D_MODEL)
    xs = x_sample.reshape(nbatch * seq_s, D_MODEL)
    k_p, v_p, k_s, v_s, conv_p, conv_s = [], [], [], [], [], []
    for i in range(DEPTH):
        li = i // 2
        g0, b0 = ln_g[i, 0:1], ln_b[i, 0:1]
        g1, b1 = ln_g[i, 1:2], ln_b[i, 1:2]
        if i % 2 == 0:
            win, wout = w_in_conv[li].astype(BF16), w_out_conv[li].astype(BF16)
            xp, cp = _conv_prompt(xp, batch, win, conv_w[li], wout, g0, b0)
            xs, cs = _conv_sample(xs, state_conv[li], seq_s, win, conv_w[li], wout, g0, b0)
            conv_p.append(cp)
            conv_s.append(cs)
        else:
            wqkv, wo = w_qkv[li].astype(BF16), w_o[li].astype(BF16)
            ktf, vtf, qt, kh, vt, kmean = _qkv_prompt(xp, batch, wqkv)
            kmean = kmean.reshape(batch, -1, N_HEADS, HEAD_DIM).transpose(0, 2, 1, 3)
            ot = _attn_prompt(slopes, qt, kh, vt, kmean)
            xp = _proj_ln(xp, ot.reshape(batch, D_MODEL, seq_p), wo, g0, b0)
            k_p.append(ktf)
            v_p.append(vtf)

            qkv = _matmul(xs, wqkv).reshape(nbatch, seq_s, 3, D_MODEL)
            q, kn, vn = qkv[:, :, 0], qkv[:, :, 1], qkv[:, :, 2]
            p, p_own = _sample_scores(page_table, cache_k, li, q, kn, slope_rows)
            o = _sample_values(page_table, cache_v, li, p, p_own, vn)
            xs = _proj_ln(xs, o.reshape(nbatch * seq_s, D_MODEL), wo, g0, b0)
            k_s.append(kn.reshape(nbatch, seq_s, N_HEADS, HEAD_DIM))
            v_s.append(vn.reshape(nbatch, seq_s, N_HEADS, HEAD_DIM))
        wgu, wd = w_gate_up[i].astype(BF16), w_down[i].astype(BF16)
        xp = _ffn_ln(xp, wgu, wd, g1, b1)
        xs = _ffn_ln(xs, wgu, wd, g1, b1)
    return (xp.reshape(batch, seq_p, D_MODEL), xs.reshape(nbatch, seq_s, D_MODEL),
            jnp.stack(k_p).transpose(0, 1, 4, 2, 3), jnp.stack(v_p).transpose(0, 1, 4, 2, 3), jnp.stack(conv_p),
            jnp.stack(k_s), jnp.stack(v_s), jnp.stack(conv_s))
```

```python
import functools

import jax
import jax.numpy as jnp
from jax import lax
from jax.experimental import pallas as pl
from jax.experimental.pallas import tpu as pltpu

D_MODEL = 1024
DEPTH = 4
N_HEADS = 16
HEAD_DIM = D_MODEL // N_HEADS
CONV_W = 3
BLOCK = 256
TOP_K = 3
PAGE_SIZE = 128
PAGES_PER_BLOCK = BLOCK // PAGE_SIZE
STEP_BLOCKS = 8
D_FF = -(-8 * D_MODEL // 768) * 256
ALPHA = (2 * DEPTH) ** 0.25
LN_EPS = 1e-5
NEG = -1e30
LOG2E = 1.4426950408889634
HEADS_PER_STEP = 2
PAIR_LANES = HEADS_PER_STEP * HEAD_DIM
V_ROWS = HEAD_DIM + 16

VMEM_LIMIT_BYTES = 56 * 1024 * 1024
ROW_TILE = 512
FF_CHUNKS = ((0, 768), (768, 1536), (1536, 2304), (2304, D_FF))

BF16 = jnp.bfloat16
F32 = jnp.float32


def _params(*sem):
    return pltpu.CompilerParams(dimension_semantics=sem, vmem_limit_bytes=VMEM_LIMIT_BYTES)


def _dot(a, b):
    return jnp.dot(a, b, preferred_element_type=F32)


def _dot_nt(a, b, precision=None):
    return lax.dot_general(a, b, (((1,), (1,)), ((), ())), precision=precision,
                           preferred_element_type=F32)


def _layer_norm(z, g, b):
    mu = jnp.mean(z, axis=-1, keepdims=True)
    zc = z - mu
    var = jnp.mean(zc * zc, axis=-1, keepdims=True)
    return zc * lax.rsqrt(var + LN_EPS) * g + b


def _resident(shape):
    zeros = (0,) * len(shape)
    return pl.BlockSpec(shape, lambda *_: zeros)


def _ffn_ln_kernel(x_ref, wgu_ref, wd_ref, g_ref, b_ref, y_ref):
    x = x_ref[...]
    xb = x.astype(BF16)
    acc = ALPHA * x
    for c0, c1 in FF_CHUNKS:
        gate = _dot(xb, wgu_ref[:, c0:c1])
        up = _dot(xb, wgu_ref[:, D_FF + c0:D_FF + c1])
        h = (gate * jax.nn.sigmoid(gate) * up).astype(BF16)
        acc = acc + _dot(h, wd_ref[c0:c1, :])
    y_ref[...] = _layer_norm(acc, g_ref[...], b_ref[...])


def _ffn_ln(x, wgu, wd, g, b):
    n = x.shape[0]
    tm = min(ROW_TILE, n)
    return pl.pallas_call(
        _ffn_ln_kernel,
        grid=(n // tm,),
        in_specs=[pl.BlockSpec((tm, D_MODEL), lambda i: (i, 0)),
                  _resident(wgu.shape), _resident(wd.shape),
                  _resident((1, D_MODEL)), _resident((1, D_MODEL))],
        out_specs=pl.BlockSpec((tm, D_MODEL), lambda i: (i, 0)),
        out_shape=jax.ShapeDtypeStruct((n, D_MODEL), F32),
        compiler_params=_params("parallel"),
        name="ffn_ln",
    )(x, wgu, wd, g, b)


def _proj_ln_kernel(x_ref, a_ref, w_ref, g_ref, b_ref, y_ref, *, a_transposed):
    a = a_ref[0].T if a_transposed else a_ref[...]
    z = ALPHA * x_ref[...] + _dot(a.astype(BF16), w_ref[...])
    y_ref[...] = _layer_norm(z, g_ref[...], b_ref[...])


def _proj_ln(x, a, w, g, b):
    n = x.shape[0]
    tm = min(ROW_TILE, n)
    if a.ndim == 3:
        tiles_per_seq = a.shape[2] // tm
        a_spec = pl.BlockSpec((1, D_MODEL, tm), lambda i: (i // tiles_per_seq, 0, i % tiles_per_seq))
    else:
        a_spec = pl.BlockSpec((tm, D_MODEL), lambda i: (i, 0))
    return pl.pallas_call(
        functools.partial(_proj_ln_kernel, a_transposed=a.ndim == 3),
        grid=(n // tm,),
        in_specs=[pl.BlockSpec((tm, D_MODEL), lambda i: (i, 0)), a_spec,
                  _resident(w.shape), _resident((1, D_MODEL)), _resident((1, D_MODEL))],
        out_specs=pl.BlockSpec((tm, D_MODEL), lambda i: (i, 0)),
        out_shape=jax.ShapeDtypeStruct((n, D_MODEL), F32),
        compiler_params=_params("parallel"),
        name="proj_ln",
    )(x, a, w, g, b)


CARRY = 8


def _conv_prompt_kernel(x_ref, win_ref, cw_ref, wout_ref, g_ref, b_ref, y_ref, state_ref, ubuf,
                        *, tm, tiles_per_seq):
    i = pl.program_id(0)
    x = x_ref[...]
    xb = x.astype(BF16)
    b_gate = _dot(xb, win_ref[:, 0:D_MODEL])
    c_gate = _dot(xb, win_ref[:, D_MODEL:2 * D_MODEL])
    h = _dot(xb, win_ref[:, 2 * D_MODEL:3 * D_MODEL])
    u = c_gate * h

    @pl.when(i % tiles_per_seq == 0)
    def _():
        ubuf[0:CARRY, :] = jnp.zeros((CARRY, D_MODEL), F32)

    @pl.when(i % tiles_per_seq != 0)
    def _():
        ubuf[0:CARRY, :] = ubuf[tm:tm + CARRY, :]

    ubuf[CARRY:CARRY + tm, :] = u
    cw = cw_ref[...]
    y = cw[2:3, :] * u
    for j in range(CONV_W - 1):
        off = CARRY - (CONV_W - 1) + j
        y = y + cw[j:j + 1, :] * ubuf[off:off + tm, :]
    m = _dot((b_gate * y).astype(BF16), wout_ref[...])
    y_ref[...] = _layer_norm(ALPHA * x + m, g_ref[...], b_ref[...])
    state_ref[0] = u[tm - (CONV_W - 1):tm, :]


def _conv_prompt(x, batch, win, cw, wout, g, b):
    n = x.shape[0]
    tm = ROW_TILE
    tiles_per_seq = n // batch // tm
    return pl.pallas_call(
        functools.partial(_conv_prompt_kernel, tm=tm, tiles_per_seq=tiles_per_seq),
        grid=(n // tm,),
        in_specs=[pl.BlockSpec((tm, D_MODEL), lambda i: (i, 0)),
                  _resident(win.shape), _resident(cw.shape), _resident(wout.shape),
                  _resident((1, D_MODEL)), _resident((1, D_MODEL))],
        out_specs=[pl.BlockSpec((tm, D_MODEL), lambda i: (i, 0)),
                   pl.BlockSpec((1, CONV_W - 1, D_MODEL), lambda i: (i // tiles_per_seq, 0, 0))],
        out_shape=[jax.ShapeDtypeStruct((n, D_MODEL), F32),
                   jax.ShapeDtypeStruct((batch, CONV_W - 1, D_MODEL), F32)],
        scratch_shapes=[pltpu.VMEM((tm + CARRY, D_MODEL), F32)],
        compiler_params=_params("arbitrary"),
        name="conv_prompt",
    )(x, win, cw, wout, g, b)


def _conv_sample_kernel(x_ref, p1_ref, p2_ref, win_ref, cw_ref, wout_ref, g_ref, b_ref,
                        y_ref, u_ref, *, seq):
    x = x_ref[...]
    n = x.shape[0]
    xb = x.astype(BF16)
    b_gate = _dot(xb, win_ref[:, 0:D_MODEL])
    c_gate = _dot(xb, win_ref[:, D_MODEL:2 * D_MODEL])
    h = _dot(xb, win_ref[:, 2 * D_MODEL:3 * D_MODEL])
    u = c_gate * h
    pos = lax.broadcasted_iota(jnp.int32, (n, D_MODEL), 0) % seq
    u1 = jnp.where(pos >= 1, pltpu.roll(u, 1, axis=0), p1_ref[...])
    u2 = jnp.where(pos >= 2, pltpu.roll(u, 2, axis=0), p2_ref[...])
    cw = cw_ref[...]
    y = cw[0:1, :] * u2 + cw[1:2, :] * u1 + cw[2:3, :] * u
    m = _dot((b_gate * y).astype(BF16), wout_ref[...])
    y_ref[...] = _layer_norm(ALPHA * x + m, g_ref[...], b_ref[...])
    u_ref[...] = u


def _conv_sample(x, state, seq, win, cw, wout, g, b):
    n = x.shape[0]
    nb = n // seq
    zeros = jnp.zeros((nb, seq, D_MODEL), F32)
    prev1 = zeros.at[:, 0].set(state[:, 1]).reshape(n, D_MODEL)
    prev2 = zeros.at[:, 0].set(state[:, 0]).at[:, 1].set(state[:, 1]).reshape(n, D_MODEL)
    full = _resident((n, D_MODEL))
    y, u = pl.pallas_call(
        functools.partial(_conv_sample_kernel, seq=seq),
        grid=(1,),
        in_specs=[full, full, full, _resident(win.shape), _resident(cw.shape), _resident(wout.shape),
                  _resident((1, D_MODEL)), _resident((1, D_MODEL))],
        out_specs=[full, full],
        out_shape=[jax.ShapeDtypeStruct((n, D_MODEL), F32)] * 2,
        compiler_params=_params("arbitrary"),
        name="conv_sample",
    )(x, prev1, prev2, win, cw, wout, g, b)
    return y, u.reshape(nb, seq, D_MODEL)[:, seq - (CONV_W - 1):]


def _qkv_prompt_kernel(x_ref, w_ref, ktf_ref, vtf_ref, qt_ref, kh_ref, vt_ref, kmean_ref):
    xb = x_ref[...].astype(BF16)
    q = _dot(xb, w_ref[:, 0:D_MODEL])
    k = _dot(xb, w_ref[:, D_MODEL:2 * D_MODEL])
    v = _dot(xb, w_ref[:, 2 * D_MODEL:3 * D_MODEL])
    kmean_ref[0, 0] = jnp.sum(k, axis=0, keepdims=True) * (1.0 / BLOCK)
    for c in range(N_HEADS // HEADS_PER_STEP):
        lanes = slice(c * PAIR_LANES, (c + 1) * PAIR_LANES)
        qt = q[:, lanes].T
        kt = k[:, lanes].T
        vt = v[:, lanes].T
        kh_ref[0, c, 0] = k[:, lanes].astype(BF16)
        for s in range(HEADS_PER_STEP):
            hd = HEADS_PER_STEP * c + s
            rows = slice(s * HEAD_DIM, (s + 1) * HEAD_DIM)
            qt_ref[0, hd] = qt[rows, :]
            ktf_ref[0, hd] = kt[rows, :]
            vtf_ref[0, hd] = vt[rows, :]
            vt_ref[0, hd, 0, 0:HEAD_DIM] = vt[rows, :].astype(BF16)
            vt_ref[0, hd, 0, HEAD_DIM:V_ROWS] = jnp.ones((V_ROWS - HEAD_DIM, BLOCK), BF16)


def _qkv_prompt(x, batch, w):
    n = x.shape[0]
    t = n // batch
    nb = t // BLOCK
    pairs = N_HEADS // HEADS_PER_STEP
    row = pl.BlockSpec((BLOCK, D_MODEL), lambda i: (i, 0))
    feature_major = pl.BlockSpec((1, N_HEADS, HEAD_DIM, BLOCK), lambda i: (i // nb, 0, 0, i % nb))
    return pl.pallas_call(
        _qkv_prompt_kernel,
        grid=(n // BLOCK,),
        in_specs=[row, _resident(w.shape)],
        out_specs=[feature_major, feature_major, feature_major,
                   pl.BlockSpec((1, pairs, 1, BLOCK, PAIR_LANES), lambda i: (i // nb, 0, i % nb, 0, 0)),
                   pl.BlockSpec((1, N_HEADS, 1, V_ROWS, BLOCK), lambda i: (i // nb, 0, i % nb, 0, 0)),
                   pl.BlockSpec((1, 1, 1, D_MODEL), lambda i: (i // nb, i % nb, 0, 0))],
        out_shape=[jax.ShapeDtypeStruct((batch, N_HEADS, HEAD_DIM, t), F32),
                   jax.ShapeDtypeStruct((batch, N_HEADS, HEAD_DIM, t), F32),
                   jax.ShapeDtypeStruct((batch, N_HEADS, HEAD_DIM, t), F32),
                   jax.ShapeDtypeStruct((batch, pairs, nb, BLOCK, PAIR_LANES), BF16),
                   jax.ShapeDtypeStruct((batch, N_HEADS, nb, V_ROWS, BLOCK), BF16),
                   jax.ShapeDtypeStruct((batch, nb, 1, D_MODEL), F32)],
        compiler_params=_params("parallel"),
        name="qkv_prompt",
    )(x, w)


def _matmul_kernel(x_ref, w_ref, y_ref):
    y_ref[...] = _dot(x_ref[...].astype(BF16), w_ref[...])


def _matmul(x, w):
    n, m = x.shape[0], w.shape[1]
    return pl.pallas_call(
        _matmul_kernel,
        grid=(1,),
        in_specs=[_resident(x.shape), _resident(w.shape)],
        out_specs=_resident((n, m)),
        out_shape=jax.ShapeDtypeStruct((n, m), F32),
        compiler_params=_params("arbitrary"),
        name="matmul",
    )(x, w)


def _select_topk(gate, blk, axis, n_blocks):
    sel = jnp.zeros(gate.shape, F32)
    for _ in range(TOP_K):
        mx = jnp.max(gate, axis=axis, keepdims=True)
        idx = jnp.min(jnp.where(gate == mx, blk, n_blocks), axis=axis, keepdims=True)
        hit = blk == idx
        sel = jnp.where(hit & (mx > -jnp.inf), 1.0, sel)
        gate = jnp.where(hit, -jnp.inf, gate)
    return sel


MASKED = 2 * NEG


def _attn_prompt_kernel(slope_ref, qa_ref, qb_ref, k_ref, vt_ref, kmean_ref, oa_ref, ob_ref,
                        sel_ref, w_ref, bias_ref, s_slots, p_slots, acc_done):
    pid = pl.program_id(2)
    nb = k_ref.shape[2]
    n_stages = nb + 1
    tiles = (pid, nb - 1 - pid)
    key = lax.broadcasted_iota(jnp.int32, (BLOCK, BLOCK), 0)
    qry = lax.broadcasted_iota(jnp.int32, (BLOCK, BLOCK), 1)
    rel = (qry - key).astype(F32)
    causal = jnp.where(key <= qry, 0.0, NEG)
    blk = lax.broadcasted_iota(jnp.int32, (nb, BLOCK), 0)
    zeros = jnp.zeros((HEAD_DIM, BLOCK), BF16)
    heads = range(HEADS_PER_STEP)

    for s in heads:
        slope2 = slope_ref[s] * LOG2E
        bias = rel * (-slope2)
        bias_ref[s, 0] = bias
        bias_ref[s, 1] = bias + causal
        for t, q_ref in enumerate((qa_ref, qb_ref)):
            i = tiles[t]
            qt = q_ref[0, s]
            gate = jnp.dot(kmean_ref[0, s], qt, precision=lax.Precision.HIGHEST,
                           preferred_element_type=F32)
            gate = jnp.where(blk < i, gate, -jnp.inf)
            sel = _select_topk(gate, blk, 0, nb)
            shift = (i - blk).astype(F32) * (-BLOCK * slope2)
            sel_ref[t, s] = jnp.where(blk == i, 0.0, jnp.where(sel > 0.0, shift, MASKED))
            qb = (qt * (HEAD_DIM ** -0.5 * LOG2E)).astype(BF16)
            pieces = [qb if r == s else zeros for r in heads]
            w_ref[t, s] = jnp.concatenate(pieces, axis=0)

    def position(k):
        in_first = k <= pid
        tile = jnp.where(in_first, 0, 1)
        block = jnp.where(in_first, k, k - pid - 1)
        own = jnp.where((k == pid) | (k == nb), 1, 0)
        return tile, block, own

    def produce(k):
        tile, block, own = position(k)
        kj = k_ref[0, 0, block]
        maxima = []
        for s in heads:
            t = _dot(kj, w_ref[tile, s]) + bias_ref[s, own]
            s_slots[k % 4, s] = t
            maxima.append(jnp.max(t, axis=0, keepdims=True))
        return maxima

    maxima = {0: produce(0), 1: produce(1)}
    m = [jnp.full((1, BLOCK), NEG, F32) for _ in heads]
    acc = [jnp.zeros((V_ROWS, BLOCK), F32) for _ in heads]
    alphas = {}
    for k in range(n_stages + 2):
        if k + 2 < n_stages:
            maxima[k + 2] = produce(k + 2)
        if k >= 2:
            tile, block, _ = position(k - 2)
            for s in heads:
                pv = _dot(vt_ref[0, s, block], p_slots[(k - 2) % 4, s])
                acc[s] = alphas[k - 2][s] * acc[s] + pv
                acc_done[tile, s] = acc[s]
        if k < n_stages:
            tile, block, _ = position(k)
            restart = k == pid + 1
            alphas[k] = []
            for s in heads:
                row = sel_ref[tile, s, pl.ds(block, 1), :]
                m_prev = jnp.where(restart, NEG, m[s])
                m_new = jnp.maximum(m_prev, maxima[k][s] + row)
                alpha = jnp.where(restart, 0.0, jnp.exp2(m_prev - m_new))
                p_slots[k % 4, s] = jnp.exp2(s_slots[k % 4, s] + (row - m_new)).astype(BF16)
                m[s] = m_new
                alphas[k].append(alpha)
            del maxima[k]
    for s in heads:
        first = acc_done[0, s]
        oa_ref[0, s] = first[0:HEAD_DIM] / first[HEAD_DIM:HEAD_DIM + 1]
        ob_ref[0, s] = acc[s][0:HEAD_DIM] / acc[s][HEAD_DIM:HEAD_DIM + 1]


def _attn_prompt(slopes, qt, kh, vt, kmean):
    batch, _, _, t = qt.shape
    nb = t // BLOCK
    assert nb % 2 == 0
    hps = HEADS_PER_STEP
    half = nb // 2
    slope_lanes = jnp.broadcast_to(slopes[:, None, None], (N_HEADS, 1, BLOCK))
    q_block = (1, hps, HEAD_DIM, BLOCK)
    first, second = pl.pallas_call(
        _attn_prompt_kernel,
        grid=(batch, N_HEADS // hps, half),
        in_specs=[pl.BlockSpec((hps, 1, BLOCK), lambda b, h, i: (h, 0, 0)),
                  pl.BlockSpec(q_block, lambda b, h, i: (b, h, 0, i)),
                  pl.BlockSpec(q_block, lambda b, h, i: (b, h, 0, nb - 1 - i)),
                  pl.BlockSpec((1, 1, nb, BLOCK, PAIR_LANES), lambda b, h, i: (b, h, 0, 0, 0)),
                  pl.BlockSpec((1, hps, nb, V_ROWS, BLOCK), lambda b, h, i: (b, h, 0, 0, 0)),
                  pl.BlockSpec((1, hps, nb, HEAD_DIM), lambda b, h, i: (b, h, 0, 0))],
        out_specs=[pl.BlockSpec(q_block, lambda b, h, i: (b, h, 0, i)),
                   pl.BlockSpec(q_block, lambda b, h, i: (b, h, 0, half - 1 - i))],
        out_shape=[jax.ShapeDtypeStruct((batch, N_HEADS, HEAD_DIM, t // 2), F32)] * 2,
        scratch_shapes=[pltpu.VMEM((2, hps, nb, BLOCK), F32),
                        pltpu.VMEM((2, hps, PAIR_LANES, BLOCK), BF16),
                        pltpu.VMEM((hps, 2, BLOCK, BLOCK), F32),
                        pltpu.VMEM((4, hps, BLOCK, BLOCK), F32),
                        pltpu.VMEM((4, hps, BLOCK, BLOCK), BF16),
                        pltpu.VMEM((2, hps, V_ROWS, BLOCK), F32)],
        compiler_params=_params("parallel", "parallel", "arbitrary"),
        name="attn_prompt",
    )(slope_lanes, qt, qt, kh, vt, kmean)
    return jnp.concatenate([first, second], axis=-1)


def _page_matrix(page_ref):
    return page_ref[0, 0].reshape(D_MODEL, PAGE_SIZE)


def _page_specs(layer, n_pages):
    per_step = STEP_BLOCKS * PAGES_PER_BLOCK

    def spec(which):
        return pl.BlockSpec((1, 1, N_HEADS, HEAD_DIM, PAGE_SIZE),
                            lambda b, j, pt: (layer, pt[b * n_pages + per_step * j + which], 0, 0, 0))

    return [spec(w) for w in range(per_step)]


def _head_rows(q4, seq):
    rows = jnp.concatenate([jnp.broadcast_to(q4[t:t + 1, :], (N_HEADS, D_MODEL)) for t in range(seq)], axis=0)
    r = lax.broadcasted_iota(jnp.int32, rows.shape, 0)
    lane = lax.broadcasted_iota(jnp.int32, rows.shape, 1)
    return jnp.where(r % N_HEADS == lane // HEAD_DIM, rows, 0.0)


def _sample_scores_kernel(pt_ref, *refs, seq, n_blocks):
    n_pages = STEP_BLOCKS * PAGES_PER_BLOCK
    page_refs = refs[:n_pages]
    q_ref, knew_ref, slope_ref, p_ref, pown_ref, s_all, kmean_t = refs[n_pages:]
    j = pl.program_id(1)
    rows = seq * N_HEADS
    qbd = _head_rows(q_ref[0], seq)
    qb = (qbd * (HEAD_DIM ** -0.5)).astype(BF16)

    @pl.when(j == 0)
    def _():
        kmean_t[...] = jnp.zeros(kmean_t.shape, F32)

    lane = lax.broadcasted_iota(jnp.int32, kmean_t.shape, 1)
    for u in range(STEP_BLOCKS):
        blk = j * STEP_BLOCKS + u
        ktot = jnp.zeros((D_MODEL, PAGE_SIZE), F32)
        for w in range(PAGES_PER_BLOCK):
            kp = _page_matrix(page_refs[u * PAGES_PER_BLOCK + w])
            ktot = ktot + kp
            s_all[blk, :, w * PAGE_SIZE:(w + 1) * PAGE_SIZE] = _dot(qb, kp.astype(BF16))
        kmean = jnp.sum(ktot, axis=1, keepdims=True) * (1.0 / BLOCK)
        kmean_t[...] = jnp.where(lane == blk, kmean, kmean_t[...])

    @pl.when(j == n_blocks // STEP_BLOCKS - 1)
    def _():
        slope = slope_ref[...]
        gate = jnp.dot(qbd, kmean_t[...], precision=lax.Precision.HIGHEST,
                       preferred_element_type=F32)
        blk = lax.broadcasted_iota(jnp.int32, gate.shape, 1)
        gate = jnp.where(blk < n_blocks, gate, -jnp.inf)
        sel = _select_topk(gate, blk, 1, n_blocks)
        past = n_blocks * BLOCK
        tok = lax.broadcasted_iota(jnp.int32, (rows, 1), 0) // N_HEADS
        lane = lax.broadcasted_iota(jnp.int32, (rows, BLOCK), 1)

        knew = knew_ref[0]
        own_lane = lax.broadcasted_iota(jnp.int32, (rows, PAGE_SIZE), 1)
        s_own = jnp.full((rows, PAGE_SIZE), NEG, F32)
        for s in range(seq):
            dots = jnp.sum(qbd * knew[s:s + 1, :], axis=1, keepdims=True) * (HEAD_DIM ** -0.5)
            logit = dots - slope * (tok - s).astype(F32)
            s_own = jnp.where((own_lane == s) & (tok >= s), logit, s_own)
        m = jnp.max(s_own, axis=1, keepdims=True)

        def logits(b):
            dist = (past + tok - b * BLOCK - lane).astype(F32)
            return jnp.where(sel[:, b:b + 1] > 0.0, s_all[b] - slope * dist, NEG)

        for b in range(n_blocks):
            m = jnp.maximum(m, jnp.max(logits(b), axis=1, keepdims=True))
        p_own = jnp.exp(s_own - m)
        l = jnp.sum(p_own, axis=1, keepdims=True)
        for b in range(n_blocks):
            l = l + jnp.sum(jnp.exp(logits(b) - m), axis=1, keepdims=True)
        inv = 1.0 / l
        pown_ref[0] = p_own * inv
        for b in range(n_blocks):
            p_ref[0, b] = (jnp.exp(logits(b) - m) * inv).astype(BF16)


def _sample_scores(page_table, cache, layer, q, knew, slope_rows):
    nbatch, seq, _ = q.shape
    n_pages = page_table.shape[1]
    n_blocks = n_pages // PAGES_PER_BLOCK
    rows = seq * N_HEADS
    grid_spec = pltpu.PrefetchScalarGridSpec(
        num_scalar_prefetch=1,
        grid=(nbatch, n_blocks // STEP_BLOCKS),
        in_specs=[*_page_specs(layer, n_pages),
                  pl.BlockSpec((1, seq, D_MODEL), lambda b, j, pt: (b, 0, 0)),
                  pl.BlockSpec((1, seq, D_MODEL), lambda b, j, pt: (b, 0, 0)),
                  pl.BlockSpec((rows, 1), lambda b, j, pt: (0, 0))],
        out_specs=[pl.BlockSpec((1, n_blocks, rows, BLOCK), lambda b, j, pt: (b, 0, 0, 0)),
                   pl.BlockSpec((1, rows, PAGE_SIZE), lambda b, j, pt: (b, 0, 0))],
        scratch_shapes=[pltpu.VMEM((n_blocks, rows, BLOCK), F32),
                        pltpu.VMEM((D_MODEL, max(n_blocks, PAGE_SIZE)), F32)],
    )
    return pl.pallas_call(
        functools.partial(_sample_scores_kernel, seq=seq, n_blocks=n_blocks),
        grid_spec=grid_spec,
        out_shape=[jax.ShapeDtypeStruct((nbatch, n_blocks, rows, BLOCK), BF16),
                   jax.ShapeDtypeStruct((nbatch, rows, PAGE_SIZE), F32)],
        compiler_params=_params("parallel", "arbitrary"),
        name="sample_scores",
    )(page_table.reshape(-1), *[cache] * (STEP_BLOCKS * PAGES_PER_BLOCK), q, knew, slope_rows)


def _sample_values_kernel(pt_ref, *refs, seq, n_blocks):
    n_pages = STEP_BLOCKS * PAGES_PER_BLOCK
    page_refs = refs[:n_pages]
    p_ref, pown_ref, vnew_ref, o_ref, acc = refs[n_pages:]
    j = pl.program_id(1)

    @pl.when(j == 0)
    def _():
        acc[...] = jnp.zeros(acc.shape, F32)

    total = jnp.zeros(acc.shape, F32)
    for u in range(STEP_BLOCKS):
        p = p_ref[0, u]
        for w in range(PAGES_PER_BLOCK):
            vp = _page_matrix(page_refs[u * PAGES_PER_BLOCK + w]).astype(BF16)
            total = total + _dot_nt(p[:, w * PAGE_SIZE:(w + 1) * PAGE_SIZE], vp)
    acc[...] += total

    @pl.when(j == n_blocks // STEP_BLOCKS - 1)
    def _():
        o = acc[...]
        p_own = pown_ref[0]
        vnew = vnew_ref[0]
        for s in range(seq):
            o = o + p_own[:, s:s + 1] * vnew[s:s + 1, :]
        r = lax.broadcasted_iota(jnp.int32, o.shape, 0)
        lane = lax.broadcasted_iota(jnp.int32, o.shape, 1)
        o = jnp.where(r % N_HEADS == lane // HEAD_DIM, o, 0.0)
        o_ref[0] = jnp.sum(o.reshape(seq, N_HEADS, D_MODEL), axis=1)


def _sample_values(page_table, cache, layer, p, p_own, vnew):
    nbatch, n_blocks, rows, _ = p.shape
    seq = rows // N_HEADS
    n_pages = page_table.shape[1]
    grid_spec = pltpu.PrefetchScalarGridSpec(
        num_scalar_prefetch=1,
        grid=(nbatch, n_blocks // STEP_BLOCKS),
        in_specs=[*_page_specs(layer, n_pages),
                  pl.BlockSpec((1, STEP_BLOCKS, rows, BLOCK), lambda b, j, pt: (b, j, 0, 0)),
                  pl.BlockSpec((1, rows, PAGE_SIZE), lambda b, j, pt: (b, 0, 0)),
                  pl.BlockSpec((1, seq, D_MODEL), lambda b, j, pt: (b, 0, 0))],
        out_specs=pl.BlockSpec((1, seq, D_MODEL), lambda b, j, pt: (b, 0, 0)),
        scratch_shapes=[pltpu.VMEM((rows, D_MODEL), F32)],
    )
    return pl.pallas_call(
        functools.partial(_sample_values_kernel, seq=seq, n_blocks=n_blocks),
        grid_spec=grid_spec,
        out_shape=jax.ShapeDtypeStruct((nbatch, seq, D_MODEL), F32),
        compiler_params=_params("parallel", "arbitrary"),
        name="sample_values",
    )(page_table.reshape(-1), *[cache] * (STEP_BLOCKS * PAGES_PER_BLOCK), p, p_own, vnew)


def kernel(x_prompt, x_sample, cache_k, cache_v, state_conv, page_table, w_in_conv, conv_w, w_out_conv,
           w_qkv, w_o, w_gate_up, w_down, ln_g, ln_b):
    batch, seq_p, _ = x_prompt.shape
    nbatch, seq_s, _ = x_sample.shape
    assert seq_p % ROW_TILE == 0 and seq_p % BLOCK == 0 and seq_s >= CONV_W - 1
    assert page_table.shape[1] % (STEP_BLOCKS * PAGES_PER_BLOCK) == 0

    slopes = 2.0 ** (-8.0 * jnp.arange(1, N_HEADS + 1, dtype=F32) / N_HEADS)
    slope_rows = jnp.tile(slopes, seq_s).reshape(seq_s * N_HEADS, 1)

    cache_k = cache_k.transpose(0, 1, 3, 4, 2)
    cache_v = cache_v.transpose(0, 1, 3, 4, 2)

    xp = x_prompt.reshape(batch * seq_p, D_MODEL)
    xs = x_sample.reshape(nbatch * seq_s, D_MODEL)
    k_p, v_p, k_s, v_s, conv_p, conv_s = [], [], [], [], [], []
    for i in range(DEPTH):
        li = i // 2
        g0, b0 = ln_g[i, 0:1], ln_b[i, 0:1]
        g1, b1 = ln_g[i, 1:2], ln_b[i, 1:2]
        if i % 2 == 0:
            win, wout = w_in_conv[li].astype(BF16), w_out_conv[li].astype(BF16)
            xp, cp = _conv_prompt(xp, batch, win, conv_w[li], wout, g0, b0)
            xs, cs = _conv_sample(xs, state_conv[li], seq_s, win, conv_w[li], wout, g0, b0)
            conv_p.append(cp)
            conv_s.append(cs)
        else:
            wqkv, wo = w_qkv[li].astype(BF16), w_o[li].astype(BF16)
            ktf, vtf, qt, kh, vt, kmean = _qkv_prompt(xp, batch, wqkv)
            kmean = kmean.reshape(batch, -1, N_HEADS, HEAD_DIM).transpose(0, 2, 1, 3)
            ot = _attn_prompt(slopes, qt, kh, vt, kmean)
            xp = _proj_ln(xp, ot.reshape(batch, D_MODEL, seq_p), wo, g0, b0)
            k_p.append(ktf)
            v_p.append(vtf)

            qkv = _matmul(xs, wqkv).reshape(nbatch, seq_s, 3, D_MODEL)
            q, kn, vn = qkv[:, :, 0], qkv[:, :, 1], qkv[:, :, 2]
            p, p_own = _sample_scores(page_table, cache_k, li, q, kn, slope_rows)
            o = _sample_values(page_table, cache_v, li, p, p_own, vn)
            xs = _proj_ln(xs, o.reshape(nbatch * seq_s, D_MODEL), wo, g0, b0)
            k_s.append(kn.reshape(nbatch, seq_s, N_HEADS, HEAD_DIM))
            v_s.append(vn.reshape(nbatch, seq_s, N_HEADS, HEAD_DIM))
        wgu, wd = w_gate_up[i].astype(BF16), w_down[i].astype(BF16)
        xp = _ffn_ln(xp, wgu, wd, g1, b1)
        xs = _ffn_ln(xs, wgu, wd, g1, b1)
    return (xp.reshape(batch, seq_p, D_MODEL), xs.reshape(nbatch, seq_s, D_MODEL),
            jnp.stack(k_p).transpose(0, 1, 4, 2, 3), jnp.stack(v_p).transpose(0, 1, 4, 2, 3), jnp.stack(conv_p),
            jnp.stack(k_s), jnp.stack(v_s), jnp.stack(conv_s))
```
